```python
import math
import jax
import jax.numpy as jnp
from jax import lax
import numpy as np


D_MODEL = 4096
BATCH = 2
SEQ = 8192
DEPTH = 1

HEAD_DIM = 128
NSA_HEADS = 16
NSA_KV_HEADS = 4
NSA_GROUP = NSA_HEADS // NSA_KV_HEADS
CMP_LEN = 32
CMP_STRIDE = 16
CMP_HIDDEN = 4 * HEAD_DIM
SEL_BLOCK = 64
SEL_TOPK = 16
WINDOW = 512
NSA_Q_BLOCK = 64
GDN_HEADS = 16
GDN_HEAD_DIM = 128
GDN_CONV = 4
GDN_CHUNK = 64
MLP_HIDDEN = 4 * D_MODEL
PLE_DIM = 256
ROPE_THETA = 10000.0
NORM_EPS = 1e-6
NEG_INF = -1e30

NSA_WIDTH = NSA_HEADS * HEAD_DIM
NSA_KV_WIDTH = NSA_KV_HEADS * HEAD_DIM
GDN_WIDTH = GDN_HEADS * GDN_HEAD_DIM
IN_SPLITS = (NSA_WIDTH, 6 * NSA_KV_WIDTH, 3 * NSA_HEADS, 3 * GDN_WIDTH, GDN_WIDTH, GDN_HEADS, GDN_HEADS, 2 * D_MODEL)
IN_WIDTH = sum(IN_SPLITS)

kernel_name = 'hybrid_nsa_gdn_gated_merge_block'


def rms_norm(x, gain):
    xf = x.astype(jnp.float32)
    y = xf * lax.rsqrt(jnp.mean(xf * xf, axis=-1, keepdims=True) + NORM_EPS)
    return (y * gain.astype(jnp.float32)).astype(x.dtype)


def l2_norm(x):
    xf = x.astype(jnp.float32)
    return xf * lax.rsqrt(jnp.sum(xf * xf, axis=-1, keepdims=True) + NORM_EPS)


def rope(x, pos):
    half = x.shape[-1] // 2
    inv_freq = ROPE_THETA ** (-jnp.arange(half, dtype=jnp.float32) / half)
    ang = pos[:, None] * inv_freq[None, :]
    cos, sin = jnp.cos(ang), jnp.sin(ang)
    xf = x.astype(jnp.float32)
    x1, x2 = xf[..., :half], xf[..., half:]
    return jnp.concatenate([x1 * cos - x2 * sin, x2 * cos + x1 * sin], axis=-1).astype(x.dtype)


def masked_softmax(s, mask):
    s = jnp.where(mask, s.astype(jnp.float32), NEG_INF)
    e = jnp.where(mask, jnp.exp(s - jnp.max(s, axis=-1, keepdims=True)), 0.0)
    return e / jnp.maximum(jnp.sum(e, axis=-1, keepdims=True), 1e-30)


def compress_blocks(t, pe, w1, w2):
    b, g, s, d = t.shape
    ch = t.reshape(b, g, s // CMP_STRIDE, CMP_STRIDE, d)
    blocks = jnp.concatenate([ch[:, :, :-1], ch[:, :, 1:]], axis=3) + pe
    hid = jax.nn.gelu(jnp.einsum('bgnld,ldh->bgnh', blocks, w1))
    return jnp.einsum('bgnh,hd->bgnd', hid, w2)


def nsa_attention(q_in, kv_in, gate_in, q_gain, kc_gain, ks_gain, kw_gain, pe_k, pe_v, wk1, wk2, wv1, wv2):
    b, s = q_in.shape[0], q_in.shape[1]
    g, r, d = NSA_KV_HEADS, NSA_GROUP, HEAD_DIM
    qbl = NSA_Q_BLOCK
    n_cmp = s // CMP_STRIDE - 1
    n_sel = s // SEL_BLOCK
    n_blk = s // qbl
    top_k = min(SEL_TOPK, n_sel)
    scale = d ** -0.5
    pos = jnp.arange(s, dtype=jnp.float32)

    q = rms_norm(q_in.reshape(b, s, g, r, d), q_gain).transpose(0, 2, 3, 1, 4)
    q = rope(q, pos)
    kv = kv_in.reshape(b, s, 6, g, d).transpose(2, 0, 3, 1, 4)
    k_c, v_c, k_s, v_s, k_w, v_w = kv[0], kv[1], kv[2], kv[3], kv[4], kv[5]

    cmp_end = jnp.arange(n_cmp) * CMP_STRIDE + (CMP_LEN - 1)
    k_cmp = rope(rms_norm(compress_blocks(k_c, pe_k, wk1, wk2), kc_gain), cmp_end.astype(jnp.float32))
    v_cmp = compress_blocks(v_c, pe_v, wv1, wv2)
    k_slc = rope(rms_norm(k_s, ks_gain), pos).reshape(b, g, n_sel, SEL_BLOCK, d)
    v_slc = v_s.reshape(b, g, n_sel, SEL_BLOCK, d)
    pad = ((0, 0), (0, 0), (WINDOW, 0), (0, 0))
    k_win = jnp.pad(rope(rms_norm(k_w, kw_gain), pos), pad)
    v_win = jnp.pad(v_w, pad)
    gates = jax.nn.sigmoid(gate_in.astype(jnp.float32)).reshape(b, s, g, r, 3).transpose(0, 2, 3, 1, 4)

    c_start = jnp.arange(n_cmp) * CMP_STRIDE
    s_start = jnp.arange(n_sel) * SEL_BLOCK
    overlap = ((c_start[:, None] < s_start[None, :] + SEL_BLOCK) & (c_start[:, None] + CMP_LEN > s_start[None, :])).astype(jnp.float32)
    b_ix = jnp.arange(b)[:, None, None, None]
    g_ix = jnp.arange(g)[None, :, None, None]
    j_blk = jnp.arange(n_sel)

    def block(n):
        t0 = n * qbl
        tq = t0 + jnp.arange(qbl)
        qb = lax.dynamic_slice_in_dim(q, t0, qbl, axis=3)
        s_c = jnp.einsum('bgrqd,bgnd->bgrqn', qb, k_cmp) * scale
        p_c = masked_softmax(s_c, cmp_end[None, :] <= tq[:, None])
        o_c = jnp.einsum('bgrqn,bgnd->bgrqd', p_c, v_cmp.astype(jnp.float32))
        imp = jnp.einsum('bgrqn,nj->bgqj', p_c, overlap)
        cur = tq // SEL_BLOCK
        valid = j_blk[None, :] <= cur[:, None]
        forced = (j_blk[None, :] == 0) | (j_blk[None, :] == cur[:, None]) | (j_blk[None, :] == cur[:, None] - 1)
        score = jnp.where(forced, jnp.inf, jnp.where(valid, imp, -jnp.inf))
        vals, idx = lax.top_k(score, top_k)
        kg = k_slc[b_ix, g_ix, idx]
        vg = v_slc[b_ix, g_ix, idx]
        key_pos = idx[..., None] * SEL_BLOCK + jnp.arange(SEL_BLOCK)
        m_s = (vals > -jnp.inf)[..., None] & (key_pos <= tq[:, None, None])
        s_s = jnp.einsum('bgrqd,bgqkld->bgrqkl', qb, kg) * scale
        n_key = top_k * SEL_BLOCK
        p_s = masked_softmax(s_s.reshape(b, g, r, qbl, n_key), m_s.reshape(b, g, 1, qbl, n_key))
        o_s = jnp.einsum('bgrqm,bgqmd->bgrqd', p_s, vg.reshape(b, g, qbl, n_key, d).astype(jnp.float32))
        kw = lax.dynamic_slice_in_dim(k_win, t0, WINDOW + qbl, axis=2)
        vw = lax.dynamic_slice_in_dim(v_win, t0, WINDOW + qbl, axis=2)
        kpos = t0 - WINDOW + jnp.arange(WINDOW + qbl)
        dist = tq[:, None] - kpos[None, :]
        m_w = (kpos[None, :] >= 0) & (dist >= 0) & (dist < WINDOW)
        s_w = jnp.einsum('bgrqd,bgkd->bgrqk', qb, kw) * scale
        p_w = masked_softmax(s_w, m_w)
        o_w = jnp.einsum('bgrqk,bgkd->bgrqd', p_w, vw.astype(jnp.float32))
        gb = lax.dynamic_slice_in_dim(gates, t0, qbl, axis=3)
        return gb[..., 0:1] * o_c + gb[..., 1:2] * o_s + gb[..., 2:3] * o_w

    out = lax.map(block, jnp.arange(n_blk))
    return out.transpose(1, 0, 4, 2, 3, 5).reshape(b, s, g * r * d)


def gated_deltanet(qkv_in, z_in, beta_in, decay_in, conv_w, a_log, dt_bias, o_gain):
    b, s = qkv_in.shape[0], qkv_in.shape[1]
    h, d, c = GDN_HEADS, GDN_HEAD_DIM, GDN_CHUNK
    n_ch = s // c
    n_feat = 3 * h * d
    conv = lax.conv_general_dilated(qkv_in, conv_w[:, None, :].astype(qkv_in.dtype), window_strides=(1,),
                                    padding=((GDN_CONV - 1, 0),), dimension_numbers=('NWC', 'WIO', 'NWC'),
                                    feature_group_count=n_feat)
    qkv = jax.nn.silu(conv.astype(jnp.float32)).reshape(b, s, 3, h, d).transpose(2, 0, 3, 1, 4)
    q = l2_norm(qkv[0]) * (d ** -0.5)
    k = l2_norm(qkv[1])
    v = qkv[2]
    beta = jax.nn.sigmoid(beta_in.astype(jnp.float32)).transpose(0, 2, 1)
    gdec = -(jnp.exp(a_log.astype(jnp.float32)) * jax.nn.softplus(decay_in.astype(jnp.float32) + dt_bias.astype(jnp.float32)))
    gdec = gdec.transpose(0, 2, 1)

    q = q.reshape(b, h, n_ch, c, d)
    k = k.reshape(b, h, n_ch, c, d)
    v = v.reshape(b, h, n_ch, c, d)
    beta = beta.reshape(b, h, n_ch, c)
    gc = jnp.cumsum(gdec.reshape(b, h, n_ch, c), axis=-1)
    tril = jnp.tril(jnp.ones((c, c), dtype=bool))
    strict = jnp.tril(jnp.ones((c, c), dtype=bool), -1)
    gamma = jnp.exp(jnp.where(tril, gc[..., :, None] - gc[..., None, :], NEG_INF))
    kb = k * beta[..., None]
    a_mat = jnp.where(strict, jnp.einsum('bhncd,bhnsd->bhncs', kb, k) * gamma, 0.0)
    eye = jnp.eye(c, dtype=jnp.float32)
    t_inv = lax.linalg.triangular_solve(eye + a_mat, jnp.broadcast_to(eye, a_mat.shape), left_side=True, lower=True, unit_diagonal=True)
    u = jnp.einsum('bhncs,bhnsd->bhncd', t_inv, v * beta[..., None])
    w = jnp.einsum('bhncs,bhnsd->bhncd', t_inv, kb * jnp.exp(gc)[..., None])
    attn = jnp.einsum('bhncd,bhnsd->bhncs', q, k) * gamma

    def step(state, inp):
        q_c, k_c, u_c, w_c, g_c, a_c = inp
        v_new = u_c - jnp.einsum('bhck,bhkv->bhcv', w_c, state)
        o_c = jnp.einsum('bhck,bhkv->bhcv', q_c * jnp.exp(g_c)[..., None], state) + jnp.einsum('bhcs,bhsv->bhcv', a_c, v_new)
        g_last = g_c[..., -1:]
        state = state * jnp.exp(g_last)[..., None] + jnp.einsum('bhck,bhcv->bhkv', k_c * jnp.exp(g_last - g_c)[..., None], v_new)
        return state, o_c

    xs = (jnp.moveaxis(q, 2, 0), jnp.moveaxis(k, 2, 0), jnp.moveaxis(u, 2, 0), jnp.moveaxis(w, 2, 0),
          jnp.moveaxis(gc, 2, 0), jnp.moveaxis(attn, 2, 0))
    state0 = jnp.zeros((b, h, d, d), dtype=jnp.float32)
    _, o = lax.scan(step, state0, xs)
    o = o.transpose(1, 2, 0, 3, 4).reshape(b, h, s, d).transpose(0, 2, 1, 3)
    z = z_in.reshape(b, s, h, d).astype(jnp.float32)
    o = rms_norm(o, o_gain) * jax.nn.silu(z)
    return o.reshape(b, s, h * d)


def setup_inputs(seed: int = 0) -> dict:
    key = jax.random.key(seed)
    keys = jax.random.split(key, 40)
    counter = [0]
    f32 = jnp.float32
    L = DEPTH

    def nxt():
        k = keys[counter[0]]
        counter[0] += 1
        return k

    def nrm(shape, scale):
        return jax.random.normal(nxt(), shape, f32) * scale

    def gain(shape):
        return 1.0 + 0.02 * jax.random.normal(nxt(), shape, f32)

    x = nrm((BATCH, SEQ, D_MODEL), 1.0)
    p = nrm((L, BATCH, SEQ, PLE_DIM), 1.0)
    g_mix = gain((L, D_MODEL))
    w_in = nrm((L, D_MODEL, IN_WIDTH), D_MODEL ** -0.5)
    nsa_q_gain = gain((L, HEAD_DIM))
    nsa_kc_gain = gain((L, HEAD_DIM))
    nsa_ks_gain = gain((L, HEAD_DIM))
    nsa_kw_gain = gain((L, HEAD_DIM))
    cmp_pe_k = nrm((L, CMP_LEN, HEAD_DIM), 0.02)
    cmp_pe_v = nrm((L, CMP_LEN, HEAD_DIM), 0.02)
    cmp_wk1 = nrm((L, CMP_LEN, HEAD_DIM, CMP_HIDDEN), (CMP_LEN * HEAD_DIM) ** -0.5)
    cmp_wk2 = nrm((L, CMP_HIDDEN, HEAD_DIM), CMP_HIDDEN ** -0.5)
    cmp_wv1 = nrm((L, CMP_LEN, HEAD_DIM, CMP_HIDDEN), (CMP_LEN * HEAD_DIM) ** -0.5)
    cmp_wv2 = nrm((L, CMP_HIDDEN, HEAD_DIM), CMP_HIDDEN ** -0.5)
    gdn_conv_w = nrm((L, GDN_CONV, 3 * GDN_WIDTH), GDN_CONV ** -0.5)
    gdn_a_log = jnp.log(jax.random.uniform(nxt(), (L, GDN_HEADS), f32, 1.0, 16.0))
    dt = jnp.exp(jax.random.uniform(nxt(), (L, GDN_HEADS), f32, math.log(1e-3), math.log(1e-1)))
    gdn_dt_bias = dt + jnp.log(-jnp.expm1(-dt))
    gdn_o_gain = gain((L, GDN_HEAD_DIM))
    w_up_nsa = nrm((L, NSA_WIDTH, D_MODEL), NSA_WIDTH ** -0.5)
    w_up_gdn = nrm((L, GDN_WIDTH, D_MODEL), GDN_WIDTH ** -0.5)
    w_out = nrm((L, D_MODEL, D_MODEL), D_MODEL ** -0.5)
    g_mlp = gain((L, D_MODEL))
    w_mlp_in = nrm((L, D_MODEL, MLP_HIDDEN), D_MODEL ** -0.5)
    w_mlp_out = nrm((L, MLP_HIDDEN, D_MODEL), MLP_HIDDEN ** -0.5)
    g_ple = gain((L, D_MODEL))
    w_ple_gate = nrm((L, D_MODEL, D_MODEL), D_MODEL ** -0.5)
    w_ple_proj = nrm((L, PLE_DIM, D_MODEL), PLE_DIM ** -0.5)
    return {'x': x, 'p': p, 'g_mix': g_mix, 'w_in': w_in,
            'nsa_q_gain': nsa_q_gain, 'nsa_kc_gain': nsa_kc_gain, 'nsa_ks_gain': nsa_ks_gain, 'nsa_kw_gain': nsa_kw_gain,
            'cmp_pe_k': cmp_pe_k, 'cmp_pe_v': cmp_pe_v, 'cmp_wk1': cmp_wk1, 'cmp_wk2': cmp_wk2,
            'cmp_wv1': cmp_wv1, 'cmp_wv2': cmp_wv2,
            'gdn_conv_w': gdn_conv_w, 'gdn_a_log': gdn_a_log, 'gdn_dt_bias': gdn_dt_bias, 'gdn_o_gain': gdn_o_gain,
            'w_up_nsa': w_up_nsa, 'w_up_gdn': w_up_gdn, 'w_out': w_out,
            'g_mlp': g_mlp, 'w_mlp_in': w_mlp_in, 'w_mlp_out': w_mlp_out,
            'g_ple': g_ple, 'w_ple_gate': w_ple_gate, 'w_ple_proj': w_ple_proj}


def reference(x, p, g_mix, w_in, nsa_q_gain, nsa_kc_gain, nsa_ks_gain, nsa_kw_gain, cmp_pe_k, cmp_pe_v,
              cmp_wk1, cmp_wk2, cmp_wv1, cmp_wv2, gdn_conv_w, gdn_a_log, gdn_dt_bias, gdn_o_gain,
              w_up_nsa, w_up_gdn, w_out, g_mlp, w_mlp_in, w_mlp_out, g_ple, w_ple_gate, w_ple_proj):
    offsets = np.cumsum(IN_SPLITS)[:-1].tolist()
    for i in range(DEPTH):
        h = rms_norm(x, g_mix[i])
        proj = h @ w_in[i]
        q_a, kv_a, gl_a, qkv_b, z_b, beta_b, decay_b, merge = jnp.split(proj, offsets, axis=-1)
        o_a = nsa_attention(q_a, kv_a, gl_a, nsa_q_gain[i], nsa_kc_gain[i], nsa_ks_gain[i], nsa_kw_gain[i],
                            cmp_pe_k[i], cmp_pe_v[i], cmp_wk1[i], cmp_wk2[i], cmp_wv1[i], cmp_wv2[i]).astype(x.dtype)
        o_b = gated_deltanet(qkv_b, z_b, beta_b, decay_b, gdn_conv_w[i], gdn_a_log[i], gdn_dt_bias[i],
                             gdn_o_gain[i]).astype(x.dtype)
        gate_a, gate_b = jnp.split(merge, 2, axis=-1)
        mixed = jax.nn.sigmoid(gate_a) * (o_a @ w_up_nsa[i]) + jax.nn.sigmoid(gate_b) * (o_b @ w_up_gdn[i])
        x = x + mixed @ w_out[i]
        hm = rms_norm(x, g_mlp[i])
        x = x + jnp.square(jax.nn.relu(hm @ w_mlp_in[i])) @ w_mlp_out[i]
        hp = rms_norm(x, g_ple[i])
        x = x + jax.nn.sigmoid(hp @ w_ple_gate[i]) * (p[i] @ w_ple_proj[i])
    return x
```

```python
import functools

import numpy as np
import jax
import jax.numpy as jnp
from jax import lax
from jax.experimental import pallas as pl
from jax.experimental.pallas import tpu as pltpu

D_MODEL = 4096
HEAD_DIM = 128
NSA_HEADS = 16
NSA_KV_HEADS = 4
NSA_GROUP = NSA_HEADS // NSA_KV_HEADS
CMP_LEN = 32
CMP_STRIDE = 16
CMP_HIDDEN = 4 * HEAD_DIM
SEL_BLOCK = 64
SEL_TOPK = 16
WINDOW = 512
GDN_HEADS = 16
GDN_CONV = 4
MLP_HIDDEN = 4 * D_MODEL
PLE_DIM = 256
ROPE_THETA = 10000.0
NORM_EPS = 1e-6
MASK_BIG = 1e30
BIAS_BIG = 2.0 ** 99

NSA_WIDTH = NSA_HEADS * HEAD_DIM
NSA_KV_WIDTH = NSA_KV_HEADS * HEAD_DIM
GDN_WIDTH = GDN_HEADS * HEAD_DIM
LANES = 128
GDN_CHUNK = 128
GDN_HEAD_BLOCK = 4
INV_BASE = 16

OFF_MERGE = 0
OFF_GQKV = OFF_MERGE + 2 * D_MODEL
OFF_NQ = OFF_GQKV + 3 * GDN_WIDTH
OFF_GZ = OFF_NQ + NSA_WIDTH
OFF_NKV = OFF_GZ + GDN_WIDTH
BIG_WIDTH = OFF_NKV + 6 * NSA_KV_WIDTH
SM_GATE = 0
SM_BETA = 3 * NSA_HEADS
SM_DECAY = SM_BETA + GDN_HEADS

V7X_VMEM_LIMIT = 56 * 1024 * 1024

F32 = jnp.float32
BF16 = jnp.bfloat16


def _cparams(sem, vmem=V7X_VMEM_LIMIT):
    return pltpu.CompilerParams(dimension_semantics=sem, vmem_limit_bytes=vmem)


def _dot(a, b):
    return jnp.dot(a, b, preferred_element_type=F32)


def _dot_nt(a, b):
    return lax.dot_general(a, b, (((1,), (1,)), ((), ())), preferred_element_type=F32)


def _dot_tn(a, b):
    return lax.dot_general(a, b, (((0,), (0,)), ((), ())), preferred_element_type=F32)


def _split3(x):
    hi = x.astype(BF16)
    r = x - hi.astype(F32)
    mid = r.astype(BF16)
    lo = (r - mid.astype(F32)).astype(BF16)
    return hi, mid, lo


def _split2(x):
    hi = x.astype(BF16)
    lo = (x - hi.astype(F32)).astype(BF16)
    return hi, lo


def _dot_hp(a, b):
    a_hi, a_lo = _split2(a)
    b_hi, b_lo = _split2(b)
    return _dot(a_hi, b_hi) + (_dot(a_hi, b_lo) + _dot(a_lo, b_hi))


def _sigmoid(x):
    return 1.0 / (1.0 + jnp.exp(-x))


def _silu(x):
    return x * _sigmoid(x)


def _iota(shape, dim):
    return lax.broadcasted_iota(jnp.int32, shape, dim)


def _rmsnorm_body(x_ref, g_ref, o_ref):
    x = x_ref[...]
    y = x * lax.rsqrt(jnp.mean(x * x, axis=-1, keepdims=True) + NORM_EPS)
    o_ref[...] = (y * g_ref[...]).astype(o_ref.dtype)


def _rmsnorm(x, gain, *, rows=256):
    m, d = x.shape
    return pl.pallas_call(
        _rmsnorm_body,
        grid=(m // rows,),
        in_specs=[pl.BlockSpec((rows, d), lambda i: (i, 0)), pl.BlockSpec((1, d), lambda i: (0, 0))],
        out_specs=pl.BlockSpec((rows, d), lambda i: (i, 0)),
        out_shape=jax.ShapeDtypeStruct((m, d), BF16),
        compiler_params=_cparams(("parallel",)),
        name="rmsnorm",
    )(x, gain.reshape(1, d))


def _mm_body(*refs, nk, n_extra, epilogue):
    a_ref, b_ref = refs[0], refs[1]
    extra = refs[2:2 + n_extra]
    o_ref = refs[2 + n_extra]
    if nk == 1:
        acc = _dot(a_ref[...], b_ref[...])
        o_ref[...] = epilogue(acc, *[e[...] for e in extra]).astype(o_ref.dtype)
    else:
        acc_ref = refs[3 + n_extra]
        k = pl.program_id(2)

        @pl.when(k == 0)
        def _():
            acc_ref[...] = jnp.zeros_like(acc_ref)

        acc_ref[...] += _dot(a_ref[...], b_ref[...])

        @pl.when(k == nk - 1)
        def _():
            o_ref[...] = epilogue(acc_ref[...], *[e[...] for e in extra]).astype(o_ref.dtype)


def _matmul(a, b, *, bm, bn, bk=None, out_dtype, epilogue=lambda acc: acc, extras=(), name):
    m, kdim = a.shape
    n = b.shape[1]
    bk = kdim if bk is None else bk
    nk = kdim // bk
    in_specs = [pl.BlockSpec((bm, bk), lambda i, j, k: (i, k)), pl.BlockSpec((bk, bn), lambda i, j, k: (k, j))]
    in_specs += [pl.BlockSpec((bm, bn), lambda i, j, k: (i, j)) for _ in extras]
    scratch = [pltpu.VMEM((bm, bn), F32)] if nk > 1 else []
    return pl.pallas_call(
        functools.partial(_mm_body, nk=nk, n_extra=len(extras), epilogue=epilogue),
        grid=(m // bm, n // bn, nk),
        in_specs=in_specs,
        out_specs=pl.BlockSpec((bm, bn), lambda i, j, k: (i, j)),
        out_shape=jax.ShapeDtypeStruct((m, n), out_dtype),
        scratch_shapes=scratch,
        compiler_params=_cparams(("parallel", "parallel", "arbitrary")),
        name=name,
    )(a, b, *extras)


def _merge_body(oa_ref, wa_ref, ob_ref, wb_ref, ga_ref, gb_ref, o_ref):
    ya = _dot(oa_ref[...], wa_ref[...])
    yb = _dot(ob_ref[...], wb_ref[...])
    o_ref[...] = (_sigmoid(ga_ref[...]) * ya + _sigmoid(gb_ref[...]) * yb).astype(o_ref.dtype)


def _merge(o_a, w_a, o_b, w_b, proj, *, bm=1024, bn=512):
    m, ka = o_a.shape
    kb = o_b.shape[1]
    n = w_a.shape[1]
    ga_blk = OFF_MERGE // bn
    gb_blk = (OFF_MERGE + D_MODEL) // bn
    return pl.pallas_call(
        _merge_body,
        grid=(m // bm, n // bn),
        in_specs=[
            pl.BlockSpec((bm, ka), lambda i, j: (i, 0)),
            pl.BlockSpec((ka, bn), lambda i, j: (0, j)),
            pl.BlockSpec((bm, kb), lambda i, j: (i, 0)),
            pl.BlockSpec((kb, bn), lambda i, j: (0, j)),
            pl.BlockSpec((bm, bn), lambda i, j: (i, ga_blk + j)),
            pl.BlockSpec((bm, bn), lambda i, j: (i, gb_blk + j)),
        ],
        out_specs=pl.BlockSpec((bm, bn), lambda i, j: (i, j)),
        out_shape=jax.ShapeDtypeStruct((m, n), BF16),
        compiler_params=_cparams(("parallel", "parallel")),
        name="merge_up",
    )(o_a, w_a, o_b, w_b, proj, proj)


def _ple_body(h_ref, wg_ref, p_ref, wp_ref, x_ref, o_ref):
    gate = _sigmoid(_dot(h_ref[...], wg_ref[...]))
    o_ref[...] = x_ref[...] + gate * _dot(p_ref[...], wp_ref[...])


def _ple(hp, w_gate, p, w_proj, x, *, bm=1024, bn=512):
    m, kd = hp.shape
    kp = p.shape[1]
    n = w_gate.shape[1]
    return pl.pallas_call(
        _ple_body,
        grid=(m // bm, n // bn),
        in_specs=[
            pl.BlockSpec((bm, kd), lambda i, j: (i, 0)),
            pl.BlockSpec((kd, bn), lambda i, j: (0, j)),
            pl.BlockSpec((bm, kp), lambda i, j: (i, 0)),
            pl.BlockSpec((kp, bn), lambda i, j: (0, j)),
            pl.BlockSpec((bm, bn), lambda i, j: (i, j)),
        ],
        out_specs=pl.BlockSpec((bm, bn), lambda i, j: (i, j)),
        out_shape=jax.ShapeDtypeStruct((m, n), F32),
        compiler_params=_cparams(("parallel", "parallel")),
        name="ple",
    )(hp, w_gate, p, w_proj, x)


def _norm_rope(x, gain, cos, sin_signed):
    y = x * lax.rsqrt(jnp.mean(x * x, axis=-1, keepdims=True) + NORM_EPS) * gain
    return y * cos + pltpu.roll(y, HEAD_DIM // 2, 1) * sin_signed


def _nsa_prep_body(q_ref, ks_ref, vs_ref, kw_ref, vw_ref, cos_ref, sin_ref, qg_ref, ksg_ref, kwg_ref,
                   qo_ref, ksa_ref, vso_ref, kwo_ref, vwo_ref, *, rows, blocks_per_seq):
    cos = cos_ref[...]
    sin = sin_ref[...]
    scale = HEAD_DIM ** -0.5
    for h in range(NSA_HEADS):
        sl = slice(h * HEAD_DIM, (h + 1) * HEAD_DIM)
        qo_ref[:, sl] = (_norm_rope(q_ref[:, sl], qg_ref[...], cos, sin) * scale).astype(BF16)
    t = (pl.program_id(0) % blocks_per_seq) * rows + _iota((rows, LANES), 0)
    onehot = (_iota((rows, LANES), 1) == t // SEL_BLOCK).astype(BF16)
    for g in range(NSA_KV_HEADS):
        sl = slice(g * HEAD_DIM, (g + 1) * HEAD_DIM)
        ksa_ref[:, 2 * g * HEAD_DIM:(2 * g + 1) * HEAD_DIM] = _norm_rope(ks_ref[:, sl], ksg_ref[...], cos, sin).astype(BF16)
        ksa_ref[:, (2 * g + 1) * HEAD_DIM:(2 * g + 2) * HEAD_DIM] = onehot
        kwo_ref[:, sl] = _norm_rope(kw_ref[:, sl], kwg_ref[...], cos, sin).astype(BF16)
    vso_ref[...] = vs_ref[...].astype(BF16)
    vwo_ref[...] = vw_ref[...].astype(BF16)


def _nsa_prep(proj, cos, sin_signed, q_gain, ks_gain, kw_gain, *, seq, rows=256):
    m = proj.shape[0]
    bps = seq // rows
    kvw = NSA_KV_WIDTH
    kv_blk = OFF_NKV // kvw

    def col(c):
        return lambda i: (i, c)

    tab = pl.BlockSpec((rows, HEAD_DIM), lambda i: (i % bps, 0))
    gain = pl.BlockSpec((1, HEAD_DIM), lambda i: (0, 0))
    return pl.pallas_call(
        functools.partial(_nsa_prep_body, rows=rows, blocks_per_seq=bps),
        grid=(m // rows,),
        in_specs=[
            pl.BlockSpec((rows, NSA_WIDTH), col(OFF_NQ // NSA_WIDTH)),
            pl.BlockSpec((rows, kvw), col(kv_blk + 2)),
            pl.BlockSpec((rows, kvw), col(kv_blk + 3)),
            pl.BlockSpec((rows, kvw), col(kv_blk + 4)),
            pl.BlockSpec((rows, kvw), col(kv_blk + 5)),
            tab, tab, gain, gain, gain,
        ],
        out_specs=[
            pl.BlockSpec((rows, NSA_WIDTH), col(0)),
            pl.BlockSpec((rows, 2 * kvw), col(0)),
            pl.BlockSpec((rows, kvw), col(0)),
            pl.BlockSpec((rows, kvw), col(0)),
            pl.BlockSpec((rows, kvw), col(0)),
        ],
        out_shape=[
            jax.ShapeDtypeStruct((m, NSA_WIDTH), BF16),
            jax.ShapeDtypeStruct((m, 2 * kvw), BF16),
            jax.ShapeDtypeStruct((m, kvw), BF16),
            jax.ShapeDtypeStruct((m, kvw), BF16),
            jax.ShapeDtypeStruct((m, kvw), BF16),
        ],
        compiler_params=_cparams(("parallel",)),
        name="nsa_prep",
    )(proj, proj, proj, proj, proj, cos, sin_signed, q_gain.reshape(1, -1), ks_gain.reshape(1, -1), kw_gain.reshape(1, -1))


def _gelu_tanh(x):
    return 0.5 * x * (1.0 + jnp.tanh(np.sqrt(2.0 / np.pi).astype(np.float32) * (x + 0.044715 * (x * x * x))))


def _compress_one(x_ref, pe_ref, w1_ref, w2_ref, n_chunks):
    half = CMP_LEN // 2
    first = jnp.zeros((n_chunks, CMP_HIDDEN), F32)
    second = jnp.zeros((n_chunks, CMP_HIDDEN), F32)
    for l in range(half):
        xl = x_ref[pl.ds(l, n_chunks, stride=CMP_STRIDE), :]
        first += _dot((xl + pe_ref[l:l + 1, :]).astype(BF16), w1_ref[l])
        second += _dot((xl + pe_ref[half + l:half + l + 1, :]).astype(BF16), w1_ref[half + l])
    hid = _gelu_tanh(first + pltpu.roll(second, n_chunks - 1, 0))
    return _dot(hid.astype(BF16), w2_ref[...])


def _compress_body(xk_ref, xv_ref, pek_ref, pev_ref, wk1_ref, wk2_ref, wv1_ref, wv2_ref, kg_ref, cos_ref, sin_ref,
                   ko_ref, vo_ref, *, n_chunks):
    k = _compress_one(xk_ref, pek_ref, wk1_ref, wk2_ref, n_chunks)
    ko_ref[...] = _norm_rope(k, kg_ref[...], cos_ref[...], sin_ref[...]).astype(BF16)
    vo_ref[...] = _compress_one(xv_ref, pev_ref, wv1_ref, wv2_ref, n_chunks).astype(BF16)


def _compress(proj, pe_k, pe_v, wk1, wk2, wv1, wv2, kc_gain, cos_c, sin_c, *, batch, seq):
    n_chunks = seq // CMP_STRIDE
    kc_blk = OFF_NKV // HEAD_DIM
    vc_blk = (OFF_NKV + NSA_KV_WIDTH) // HEAD_DIM

    def full(shape):
        return pl.BlockSpec(shape, lambda b, g: (0,) * len(shape))

    out_spec = pl.BlockSpec((None, None, n_chunks, HEAD_DIM), lambda b, g: (b, g, 0, 0))
    out_shape = jax.ShapeDtypeStruct((batch, NSA_KV_HEADS, n_chunks, HEAD_DIM), BF16)
    return pl.pallas_call(
        functools.partial(_compress_body, n_chunks=n_chunks),
        grid=(batch, NSA_KV_HEADS),
        in_specs=[
            pl.BlockSpec((seq, HEAD_DIM), lambda b, g: (b, kc_blk + g)),
            pl.BlockSpec((seq, HEAD_DIM), lambda b, g: (b, vc_blk + g)),
            full((CMP_LEN, HEAD_DIM)), full((CMP_LEN, HEAD_DIM)),
            full((CMP_LEN, HEAD_DIM, CMP_HIDDEN)), full((CMP_HIDDEN, HEAD_DIM)),
            full((CMP_LEN, HEAD_DIM, CMP_HIDDEN)), full((CMP_HIDDEN, HEAD_DIM)),
            full((1, HEAD_DIM)), full((n_chunks, HEAD_DIM)), full((n_chunks, HEAD_DIM)),
        ],
        out_specs=[out_spec, out_spec],
        out_shape=[out_shape, out_shape],
        compiler_params=_cparams(("parallel", "parallel")),
        name="nsa_compress",
    )(proj, proj, pe_k, pe_v, wk1, wk2, wv1, wv2, kc_gain.reshape(1, -1), cos_c, sin_c)


def _group_rows(q_ref):
    return jnp.concatenate([q_ref[:, r * HEAD_DIM:(r + 1) * HEAD_DIM] for r in range(NSA_GROUP)], axis=0)


def _group_positions(t0, tq):
    return t0 + (_iota((NSA_GROUP * tq, 1), 0) & (tq - 1))


def _select_body(q_ref, kc_ref, vc_ref, ov_ref, oc_ref, sb_ref, *, tq, top_k):
    t0 = pl.program_id(2) * tq
    n_cmp = kc_ref.shape[0]
    q4 = _group_rows(q_ref)
    tpos4 = _group_positions(t0, tq)
    s = _dot_nt(q4, kc_ref[...])
    mask = _iota((1, n_cmp), 1) * CMP_STRIDE + (CMP_LEN - 1) <= tpos4
    s = jnp.where(mask, s, -MASK_BIG)
    e = jnp.where(mask, jnp.exp(s - jnp.max(s, axis=-1, keepdims=True)), 0.0)
    p = e / jnp.maximum(jnp.sum(e, axis=-1, keepdims=True), 1e-30)
    oc = _dot(p.astype(BF16), vc_ref[...])
    for r in range(NSA_GROUP):
        oc_ref[:, r * HEAD_DIM:(r + 1) * HEAD_DIM] = oc[r * tq:(r + 1) * tq]
    psum = p[0:tq]
    for r in range(1, NSA_GROUP):
        psum = psum + p[r * tq:(r + 1) * tq]
    ov = ov_ref[...]
    p_hi, p_mid, p_lo = _split3(psum)
    imp = _dot(p_hi, ov) + (_dot(p_mid, ov) + _dot(p_lo, ov))
    imp_t = imp.T
    n_lane = imp_t.shape[0]
    j = _iota((n_lane, tq), 0)
    cur = (t0 + _iota((n_lane, tq), 1)) // SEL_BLOCK
    valid = j <= cur
    forced = (j == 0) | (j == cur) | (j == cur - 1)
    score = jnp.where(forced, jnp.inf, jnp.where(valid, imp_t, -jnp.inf))
    sel = jnp.zeros((n_lane, tq), F32)
    for _ in range(top_k):
        best = jnp.max(score, axis=0, keepdims=True)
        first = jnp.min(jnp.where(score == best, j, n_lane), axis=0, keepdims=True)
        hit = j == first
        sel = jnp.where(hit, 1.0, sel)
        score = jnp.where(hit, -jnp.inf, score)
    sel = jnp.where(valid, sel, 0.0)
    sb_ref[...] = ((sel.T - 1.0) * BIAS_BIG).astype(BF16)


def _select(qn, k_cmp, v_cmp, overlap, *, batch, seq, tq=128):
    m = qn.shape[0]
    n_cmp = k_cmp.shape[2]
    gw = NSA_GROUP * HEAD_DIM
    nq = seq // tq
    top_k = min(SEL_TOPK, seq // SEL_BLOCK)
    cmp_spec = pl.BlockSpec((None, None, n_cmp, HEAD_DIM), lambda b, g, i: (b, g, 0, 0))
    return pl.pallas_call(
        functools.partial(_select_body, tq=tq, top_k=top_k),
        grid=(batch, NSA_KV_HEADS, nq),
        in_specs=[
            pl.BlockSpec((tq, gw), lambda b, g, i: (b * nq + i, g)),
            cmp_spec, cmp_spec,
            pl.BlockSpec((n_cmp, LANES), lambda b, g, i: (0, 0)),
        ],
        out_specs=[
            pl.BlockSpec((tq, gw), lambda b, g, i: (b * nq + i, g)),
            pl.BlockSpec((tq, LANES), lambda b, g, i: (b * nq + i, g)),
        ],
        out_shape=[
            jax.ShapeDtypeStruct((m, NSA_WIDTH), F32),
            jax.ShapeDtypeStruct((m, NSA_KV_HEADS * LANES), BF16),
        ],
        compiler_params=_cparams(("parallel", "parallel", "parallel")),
        name="nsa_select",
    )(qn, k_cmp, v_cmp, overlap)


def _lane_column(x, lane):
    return jnp.sum(jnp.where(_iota(x.shape, 1) == lane, x, 0.0), axis=1, keepdims=True)


def _attend_body(q_ref, sb_ref, ksa_ref, vs_ref, kw_ref, vw_ref, oc_ref, sm_ref, o_ref, m_scr, l_scr, acc_scr,
                 *, tq, tk):
    g = pl.program_id(1)
    t0 = pl.program_id(2) * tq
    q4 = _group_rows(q_ref)
    tpos4 = _group_positions(t0, tq)

    qa = jnp.concatenate([q4, jnp.concatenate([sb_ref[...]] * NSA_GROUP, axis=0)], axis=1)
    m_scr[...] = jnp.full(m_scr.shape, -MASK_BIG, F32)
    l_scr[...] = jnp.zeros(l_scr.shape, F32)
    acc_scr[...] = jnp.zeros(acc_scr.shape, F32)

    def tile(kt, causal):
        k0 = pl.multiple_of(kt * tk, tk)
        s = _dot_nt(qa, ksa_ref[pl.ds(k0, tk), :])
        if causal:
            s = jnp.where(k0 + _iota((1, tk), 1) <= tpos4, s, -MASK_BIG)
        m_prev = m_scr[...]
        m_new = jnp.maximum(m_prev, jnp.max(s, axis=-1, keepdims=True))
        alpha = jnp.exp(m_prev - m_new)
        p = jnp.exp(s - m_new)
        l_scr[...] = alpha * l_scr[...] + jnp.sum(p, axis=-1, keepdims=True)
        acc_scr[...] = alpha * acc_scr[...] + _dot(p.astype(BF16), vs_ref[pl.ds(k0, tk), :])
        m_scr[...] = m_new

    n_full = t0 // tk

    def body(kt, carry):
        tile(kt, False)
        return carry

    lax.fori_loop(0, n_full, body, 0)
    tile(n_full, True)
    o_s = acc_scr[...] / jnp.maximum(l_scr[...], 1e-30)

    band = WINDOW + tq
    s0 = pl.multiple_of(jnp.maximum(t0 - WINDOW, 0), tq)
    s = _dot_nt(q4, kw_ref[pl.ds(s0, band), :])
    dist = tpos4 - (s0 + _iota((1, band), 1))
    mask = (dist >= 0) & (dist < WINDOW)
    s = jnp.where(mask, s, -MASK_BIG)
    e = jnp.where(mask, jnp.exp(s - jnp.max(s, axis=-1, keepdims=True)), 0.0)
    p = e / jnp.maximum(jnp.sum(e, axis=-1, keepdims=True), 1e-30)
    o_w = _dot(p.astype(BF16), vw_ref[pl.ds(s0, band), :])

    gates = _sigmoid(sm_ref[...])
    for r in range(NSA_GROUP):
        lane = SM_GATE + (g * NSA_GROUP + r) * 3
        rows = slice(r * tq, (r + 1) * tq)
        out = (_lane_column(gates, lane) * oc_ref[:, r * HEAD_DIM:(r + 1) * HEAD_DIM]
               + _lane_column(gates, lane + 1) * o_s[rows] + _lane_column(gates, lane + 2) * o_w[rows])
        o_ref[:, r * HEAD_DIM:(r + 1) * HEAD_DIM] = out.astype(BF16)


def _attend(qn, selbias, ksa, vs, kw, vw, o_c, small, *, batch, seq, tq=128, tk=512):
    m = qn.shape[0]
    gw = NSA_GROUP * HEAD_DIM
    nq = seq // tq
    tk = min(tk, seq)

    def tile_spec(width):
        return pl.BlockSpec((tq, width), lambda b, g, i: (b * nq + i, g))

    def seq_spec(width):
        return pl.BlockSpec((seq, width), lambda b, g, i: (b, g))

    rows = NSA_GROUP * tq
    return pl.pallas_call(
        functools.partial(_attend_body, tq=tq, tk=tk),
        grid=(batch, NSA_KV_HEADS, nq),
        in_specs=[
            tile_spec(gw), tile_spec(LANES),
            seq_spec(2 * HEAD_DIM), seq_spec(HEAD_DIM), seq_spec(HEAD_DIM), seq_spec(HEAD_DIM),
            tile_spec(gw),
            pl.BlockSpec((tq, LANES), lambda b, g, i: (b * nq + i, 0)),
        ],
        out_specs=tile_spec(gw),
        out_shape=jax.ShapeDtypeStruct((m, NSA_WIDTH), BF16),
        scratch_shapes=[pltpu.VMEM((rows, 1), F32), pltpu.VMEM((rows, 1), F32), pltpu.VMEM((rows, HEAD_DIM), F32)],
        compiler_params=_cparams(("parallel", "parallel", "arbitrary")),
        name="nsa_attend",
    )(qn, selbias, ksa, vs, kw, vw, o_c, small)


def _rope_tables(pos):
    half = HEAD_DIM // 2
    inv_freq = ROPE_THETA ** (-jnp.arange(half, dtype=F32) / half)
    ang = pos[:, None] * inv_freq[None, :]
    cos, sin = jnp.cos(ang), jnp.sin(ang)
    return jnp.concatenate([cos, cos], axis=-1), jnp.concatenate([-sin, sin], axis=-1)


def _overlap_matrix(n_cmp_pad, n_cmp):
    c_start = np.arange(n_cmp_pad) * CMP_STRIDE
    s_start = np.arange(LANES) * SEL_BLOCK
    ov = (c_start[:, None] < s_start[None, :] + SEL_BLOCK) & (c_start[:, None] + CMP_LEN > s_start[None, :])
    ov &= (np.arange(n_cmp_pad) < n_cmp)[:, None]
    return jnp.asarray(ov.astype(np.float32), dtype=BF16)


def _nsa(proj, small, q_gain, kc_gain, ks_gain, kw_gain, pe_k, pe_v, wk1, wk2, wv1, wv2, *, batch, seq):
    n_chunks = seq // CMP_STRIDE
    cos, sin = _rope_tables(jnp.arange(seq, dtype=F32))
    cmp_end = jnp.arange(n_chunks) * CMP_STRIDE + (CMP_LEN - 1)
    cos_c, sin_c = _rope_tables(cmp_end.astype(F32))
    qn, ksa, vs, kw, vw = _nsa_prep(proj, cos, sin, q_gain, ks_gain, kw_gain, seq=seq)
    k_cmp, v_cmp = _compress(proj, pe_k, pe_v, wk1.astype(BF16), wk2.astype(BF16), wv1.astype(BF16), wv2.astype(BF16),
                             kc_gain, cos_c, sin_c, batch=batch, seq=seq)
    o_c, selbias = _select(qn, k_cmp, v_cmp, _overlap_matrix(n_chunks, n_chunks - 1), batch=batch, seq=seq)
    return _attend(qn, selbias, ksa, vs, kw, vw, o_c, small, batch=batch, seq=seq)


def _unit_lower_inverse(a):
    c = a.shape[0]
    row = _iota((c, c), 0)
    col = _iota((c, c), 1)
    eye = (row == col).astype(F32)
    x = jnp.where(row // INV_BASE == col // INV_BASE, -a, 0.0)
    t = eye + x
    power = _dot_hp(x, x)
    span = 2
    while span < INV_BASE:
        if 2 * span < INV_BASE:
            both = _dot_hp(power, jnp.concatenate([power, t], axis=1))
            t = t + both[:, c:]
            power = both[:, :c]
        else:
            t = t + _dot_hp(power, t)
        span *= 2
    size = INV_BASE
    while size < c:
        off = (row // (2 * size) == col // (2 * size)) & (row // size != col // size)
        t = t - _dot_hp(t, _dot_hp(jnp.where(off, a, 0.0), t))
        size *= 2
    return t


def _gdn_body(xq_ref, xk_ref, xv_ref, z_ref, sm_ref, wq_ref, wk_ref, wv_ref, alog_ref, dtb_ref, og_ref, o_ref,
              xs_scr, state_scr, *, heads):
    c = GDN_CHUNK
    hb = pl.program_id(1)
    step = pl.program_id(2)
    halo = 8

    @pl.when(step == 0)
    def _():
        xs_scr[:, 0:halo, :] = jnp.zeros((3, halo, xs_scr.shape[2]), F32)
        state_scr[...] = jnp.zeros_like(state_scr)

    conv = []
    for idx, (x_ref, w_ref) in enumerate(((xq_ref, wq_ref), (xk_ref, wk_ref), (xv_ref, wv_ref))):
        xs_scr[idx, halo:halo + c, :] = x_ref[...]
        y = jnp.zeros(x_ref.shape, F32)
        for tap in range(GDN_CONV):
            y = y + w_ref[tap:tap + 1, :] * xs_scr[idx, pl.ds(halo - (GDN_CONV - 1) + tap, c), :]
        xs_scr[idx, 0:halo, :] = xs_scr[idx, c:c + halo, :]
        conv.append(_silu(y))
    qc, kc, vc = conv

    sm = sm_ref[...]
    beta_all = _sigmoid(sm)
    x = sm + dtb_ref[...]
    softplus = jnp.maximum(x, 0.0) + jnp.log1p(jnp.exp(-jnp.abs(x)))
    g_all = -(jnp.exp(alog_ref[...]) * softplus)
    row = _iota((c, c), 0)
    col = _iota((c, c), 1)
    tril = row >= col
    ones_tril = tril.astype(BF16)
    g_hi, g_mid, g_lo = _split3(g_all)
    gc_all = _dot(ones_tril, g_hi) + (_dot(ones_tril, g_mid) + _dot(ones_tril, g_lo))
    gc_all_t = gc_all.T

    for h in range(heads):
        sl = slice(h * HEAD_DIM, (h + 1) * HEAD_DIM)
        head = hb * heads + h
        q = qc[:, sl]
        k = kc[:, sl]
        v = vc[:, sl]
        q = q * lax.rsqrt(jnp.sum(q * q, axis=-1, keepdims=True) + NORM_EPS) * (HEAD_DIM ** -0.5)
        k = k * lax.rsqrt(jnp.sum(k * k, axis=-1, keepdims=True) + NORM_EPS)
        beta = _lane_column(beta_all, SM_BETA + head)
        g_col = _lane_column(gc_all, SM_DECAY + head)
        g_row = jnp.sum(jnp.where(_iota((LANES, c), 0) == SM_DECAY + head, gc_all_t, 0.0), axis=0, keepdims=True)
        g_last = g_row[:, c - 1:c]
        gamma = jnp.exp(jnp.where(tril, g_col - g_row, -MASK_BIG))
        decay = jnp.exp(g_col)
        kb = k * beta
        k16 = k.astype(BF16)
        kk_qk = _dot_nt(jnp.concatenate([kb, q], axis=0).astype(BF16), k16)
        a_mat = jnp.where(row > col, kk_qk[:c] * gamma, 0.0)
        attn = kk_qk[c:] * gamma
        t_inv = _unit_lower_inverse(a_mat)
        uw = _dot(t_inv.astype(BF16), jnp.concatenate([v * beta, kb * decay], axis=1).astype(BF16))
        state = state_scr[h]
        ws_qs = _dot(jnp.concatenate([uw[:, HEAD_DIM:], q * decay], axis=0).astype(BF16), state.astype(BF16))
        v_new = uw[:, :HEAD_DIM] - ws_qs[:c]
        v16 = v_new.astype(BF16)
        o = ws_qs[c:] + _dot(attn.astype(BF16), v16)
        k_tail = (k * jnp.exp(g_last - g_col)).astype(BF16)
        state_scr[h] = state * jnp.exp(g_last) + _dot_tn(k_tail, v16)
        o = o * lax.rsqrt(jnp.mean(o * o, axis=-1, keepdims=True) + NORM_EPS) * og_ref[...]
        o_ref[:, sl] = (o * _silu(z_ref[:, sl])).astype(BF16)


def _lane_row(values, offset):
    return jnp.zeros((1, LANES), F32).at[0, offset:offset + values.shape[0]].set(values)


def _gdn(proj, small, conv_w, a_log, dt_bias, o_gain, *, batch, seq):
    m = proj.shape[0]
    c = GDN_CHUNK
    heads = GDN_HEAD_BLOCK
    width = heads * HEAD_DIM
    n_steps = seq // c

    def act_spec(off):
        base = off // width
        return pl.BlockSpec((c, width), lambda b, hb, s: (b * n_steps + s, base + hb))

    def conv_spec(part):
        base = part * GDN_WIDTH // width
        return pl.BlockSpec((GDN_CONV, width), lambda b, hb, s: (0, base + hb))

    row_spec = pl.BlockSpec((1, LANES), lambda b, hb, s: (0, 0))
    return pl.pallas_call(
        functools.partial(_gdn_body, heads=heads),
        grid=(batch, GDN_HEADS // heads, n_steps),
        in_specs=[
            act_spec(OFF_GQKV), act_spec(OFF_GQKV + GDN_WIDTH), act_spec(OFF_GQKV + 2 * GDN_WIDTH), act_spec(OFF_GZ),
            pl.BlockSpec((c, LANES), lambda b, hb, s: (b * n_steps + s, 0)),
            conv_spec(0), conv_spec(1), conv_spec(2),
            row_spec, row_spec, row_spec,
        ],
        out_specs=pl.BlockSpec((c, width), lambda b, hb, s: (b * n_steps + s, hb)),
        out_shape=jax.ShapeDtypeStruct((m, GDN_WIDTH), BF16),
        scratch_shapes=[pltpu.VMEM((3, c + 8, width), F32), pltpu.VMEM((heads, HEAD_DIM, HEAD_DIM), F32)],
        compiler_params=_cparams(("parallel", "parallel", "arbitrary")),
        name="gdn",
    )(proj, proj, proj, proj, small, conv_w, conv_w, conv_w,
      _lane_row(a_log, SM_DECAY), _lane_row(dt_bias, SM_DECAY), o_gain.reshape(1, -1))


def _split_w_in(w_in):
    sizes = (NSA_WIDTH, 6 * NSA_KV_WIDTH, 3 * NSA_HEADS, 3 * GDN_WIDTH, GDN_WIDTH, GDN_HEADS, GDN_HEADS, 2 * D_MODEL)
    offs = np.cumsum((0,) + sizes)
    nq, nkv, ngate, gqkv, gz, gbeta, gdecay, merge = (w_in[:, offs[i]:offs[i + 1]] for i in range(len(sizes)))
    big = jnp.concatenate([merge, gqkv, nq, gz, nkv], axis=1).astype(BF16)
    pad = jnp.zeros((w_in.shape[0], LANES - (3 * NSA_HEADS + 2 * GDN_HEADS)), w_in.dtype)
    small = jnp.concatenate([ngate, gbeta, gdecay, pad], axis=1).astype(BF16)
    return big, small


def _relu2(acc):
    r = jnp.maximum(acc, 0.0)
    return r * r


def _add(acc, res):
    return res + acc


def _layer(x, p, g_mix, w_in, nsa_q_gain, nsa_kc_gain, nsa_ks_gain, nsa_kw_gain, cmp_pe_k, cmp_pe_v, cmp_wk1, cmp_wk2,
           cmp_wv1, cmp_wv2, gdn_conv_w, gdn_a_log, gdn_dt_bias, gdn_o_gain, w_up_nsa, w_up_gdn, w_out, g_mlp,
           w_mlp_in, w_mlp_out, g_ple, w_ple_gate, w_ple_proj, *, batch, seq):
    h = _rmsnorm(x, g_mix)
    w_big, w_small = _split_w_in(w_in)
    proj = _matmul(h, w_big, bm=1024, bn=512, out_dtype=F32, name="in_proj")
    small = _matmul(h, w_small, bm=1024, bn=LANES, out_dtype=F32, name="in_proj_small")
    o_a = _nsa(proj, small, nsa_q_gain, nsa_kc_gain, nsa_ks_gain, nsa_kw_gain, cmp_pe_k, cmp_pe_v, cmp_wk1, cmp_wk2,
               cmp_wv1, cmp_wv2, batch=batch, seq=seq)
    o_b = _gdn(proj, small, gdn_conv_w, gdn_a_log, gdn_dt_bias, gdn_o_gain, batch=batch, seq=seq)
    mixed = _merge(o_a, w_up_nsa.astype(BF16), o_b, w_up_gdn.astype(BF16), proj)
    x = _matmul(mixed, w_out.astype(BF16), bm=1024, bn=512, out_dtype=F32, epilogue=_add, extras=(x,), name="out_proj")
    hm = _rmsnorm(x, g_mlp)
    hidden = _matmul(hm, w_mlp_in.astype(BF16), bm=1024, bn=512, out_dtype=BF16, epilogue=_relu2, name="mlp_in")
    x = _matmul(hidden, w_mlp_out.astype(BF16), bm=1024, bn=1024, bk=2048, out_dtype=F32, epilogue=_add, extras=(x,),
                name="mlp_out")
    hp = _rmsnorm(x, g_ple)
    return _ple(hp, w_ple_gate.astype(BF16), p.astype(BF16), w_ple_proj.astype(BF16), x)


def kernel(x, p, g_mix, w_in, nsa_q_gain, nsa_kc_gain, nsa_ks_gain, nsa_kw_gain, cmp_pe_k, cmp_pe_v, cmp_wk1, cmp_wk2,
           cmp_wv1, cmp_wv2, gdn_conv_w, gdn_a_log, gdn_dt_bias, gdn_o_gain, w_up_nsa, w_up_gdn, w_out, g_mlp,
           w_mlp_in, w_mlp_out, g_ple, w_ple_gate, w_ple_proj):
    batch, seq, d = x.shape
    depth = p.shape[0]
    assert seq // SEL_BLOCK <= LANES and seq % (4 * LANES) == 0
    y = x.reshape(batch * seq, d)
    for i in range(depth):
        y = _layer(y, p[i].reshape(batch * seq, -1), g_mix[i], w_in[i], nsa_q_gain[i], nsa_kc_gain[i], nsa_ks_gain[i],
                   nsa_kw_gain[i], cmp_pe_k[i], cmp_pe_v[i], cmp_wk1[i], cmp_wk2[i], cmp_wv1[i], cmp_wv2[i],
                   gdn_conv_w[i], gdn_a_log[i], gdn_dt_bias[i], gdn_o_gain[i], w_up_nsa[i], w_up_gdn[i], w_out[i],
                   g_mlp[i], w_mlp_in[i], w_mlp_out[i], g_ple[i], w_ple_gate[i], w_ple_proj[i], batch=batch, seq=seq)
    return y.reshape(batch, seq, d)
```

```python
import functools

import numpy as np
import jax
import jax.numpy as jnp
from jax import lax
from jax.experimental import pallas as pl
from jax.experimental.pallas import tpu as pltpu

D_MODEL = 4096
HEAD_DIM = 128
NSA_HEADS = 16
NSA_KV_HEADS = 4
NSA_GROUP = NSA_HEADS // NSA_KV_HEADS
CMP_LEN = 32
CMP_STRIDE = 16
CMP_HIDDEN = 4 * HEAD_DIM
SEL_BLOCK = 64
SEL_TOPK = 16
WINDOW = 512
GDN_HEADS = 16
GDN_CONV = 4
MLP_HIDDEN = 4 * D_MODEL
PLE_DIM = 256
ROPE_THETA = 10000.0
NORM_EPS = 1e-6
MASK_BIG = 1e30
BIAS_BIG = 2.0 ** 99
LOG2_E = 1.4426950408889634

NSA_WIDTH = NSA_HEADS * HEAD_DIM
NSA_KV_WIDTH = NSA_KV_HEADS * HEAD_DIM
GDN_WIDTH = GDN_HEADS * HEAD_DIM
LANES = 128
GDN_CHUNK = 128
GDN_HEAD_BLOCK = 8
INV_BASE = 16

OFF_MERGE = 0
OFF_GQKV = OFF_MERGE + 2 * D_MODEL
OFF_NQ = OFF_GQKV + 3 * GDN_WIDTH
OFF_GZ = OFF_NQ + NSA_WIDTH
OFF_NKV = OFF_GZ + GDN_WIDTH
BIG_WIDTH = OFF_NKV + 6 * NSA_KV_WIDTH
SM_GATE = 0
SM_BETA = 3 * NSA_HEADS
SM_DECAY = SM_BETA + GDN_HEADS

V7X_VMEM_LIMIT = 56 * 1024 * 1024

F32 = jnp.float32
BF16 = jnp.bfloat16


def _cparams(sem, vmem=V7X_VMEM_LIMIT):
    return pltpu.CompilerParams(dimension_semantics=sem, vmem_limit_bytes=vmem)


def _dot(a, b):
    return jnp.dot(a, b, preferred_element_type=F32)


def _dot_nt(a, b):
    return lax.dot_general(a, b, (((1,), (1,)), ((), ())), preferred_element_type=F32)


def _dot_tn(a, b):
    return lax.dot_general(a, b, (((0,), (0,)), ((), ())), preferred_element_type=F32)


def _split3(x):
    hi = x.astype(BF16)
    r = x - hi.astype(F32)
    mid = r.astype(BF16)
    lo = (r - mid.astype(F32)).astype(BF16)
    return hi, mid, lo


def _split2(x):
    hi = x.astype(BF16)
    lo = (x - hi.astype(F32)).astype(BF16)
    return hi, lo


def _dot_hp(a, b):
    return _dot(a[0], b[0]) + (_dot(a[0], b[1]) + _dot(a[1], b[0]))


def _sigmoid(x):
    return 1.0 / (1.0 + jnp.exp(-x))


def _silu(x):
    return x * _sigmoid(x)


def _iota(shape, dim):
    return lax.broadcasted_iota(jnp.int32, shape, dim)


def _rmsnorm_body(x_ref, g_ref, o_ref):
    x = x_ref[...]
    y = x * lax.rsqrt(jnp.mean(x * x, axis=-1, keepdims=True) + NORM_EPS)
    o_ref[...] = (y * g_ref[...]).astype(o_ref.dtype)


def _rmsnorm(x, gain, *, rows=256):
    m, d = x.shape
    return pl.pallas_call(
        _rmsnorm_body,
        grid=(m // rows,),
        in_specs=[pl.BlockSpec((rows, d), lambda i: (i, 0)), pl.BlockSpec((1, d), lambda i: (0, 0))],
        out_specs=pl.BlockSpec((rows, d), lambda i: (i, 0)),
        out_shape=jax.ShapeDtypeStruct((m, d), BF16),
        compiler_params=_cparams(("parallel",)),
        name="rmsnorm",
    )(x, gain.reshape(1, d))


def _mm_body(*refs, nk, n_extra, epilogue):
    a_ref, b_ref = refs[0], refs[1]
    extra = refs[2:2 + n_extra]
    o_ref = refs[2 + n_extra]
    if nk == 1:
        acc = _dot(a_ref[...], b_ref[...])
        o_ref[...] = epilogue(acc, *[e[...] for e in extra]).astype(o_ref.dtype)
    else:
        acc_ref = refs[3 + n_extra]
        k = pl.program_id(2)

        @pl.when(k == 0)
        def _():
            acc_ref[...] = jnp.zeros_like(acc_ref)

        acc_ref[...] += _dot(a_ref[...], b_ref[...])

        @pl.when(k == nk - 1)
        def _():
            o_ref[...] = epilogue(acc_ref[...], *[e[...] for e in extra]).astype(o_ref.dtype)


def _matmul(a, b, *, bm, bn, bk=None, out_dtype, epilogue=lambda acc: acc, extras=(), name):
    m, kdim = a.shape
    n = b.shape[1]
    bk = kdim if bk is None else bk
    nk = kdim // bk
    in_specs = [pl.BlockSpec((bm, bk), lambda i, j, k: (i, k)), pl.BlockSpec((bk, bn), lambda i, j, k: (k, j))]
    in_specs += [pl.BlockSpec((bm, bn), lambda i, j, k: (i, j)) for _ in extras]
    scratch = [pltpu.VMEM((bm, bn), F32)] if nk > 1 else []
    return pl.pallas_call(
        functools.partial(_mm_body, nk=nk, n_extra=len(extras), epilogue=epilogue),
        grid=(m // bm, n // bn, nk),
        in_specs=in_specs,
        out_specs=pl.BlockSpec((bm, bn), lambda i, j, k: (i, j)),
        out_shape=jax.ShapeDtypeStruct((m, n), out_dtype),
        scratch_shapes=scratch,
        compiler_params=_cparams(("parallel", "parallel", "arbitrary")),
        name=name,
    )(a, b, *extras)


def _merge_body(oa_ref, wa_ref, ob_ref, wb_ref, ga_ref, gb_ref, o_ref):
    ya = _dot(oa_ref[...], wa_ref[...])
    yb = _dot(ob_ref[...], wb_ref[...])
    o_ref[...] = (_sigmoid(ga_ref[...]) * ya + _sigmoid(gb_ref[...]) * yb).astype(o_ref.dtype)


def _merge(o_a, w_a, o_b, w_b, proj, *, bm=1024, bn=512):
    m, ka = o_a.shape
    kb = o_b.shape[1]
    n = w_a.shape[1]
    ga_blk = OFF_MERGE // bn
    gb_blk = (OFF_MERGE + D_MODEL) // bn
    return pl.pallas_call(
        _merge_body,
        grid=(m // bm, n // bn),
        in_specs=[
            pl.BlockSpec((bm, ka), lambda i, j: (i, 0)),
            pl.BlockSpec((ka, bn), lambda i, j: (0, j)),
            pl.BlockSpec((bm, kb), lambda i, j: (i, 0)),
            pl.BlockSpec((kb, bn), lambda i, j: (0, j)),
            pl.BlockSpec((bm, bn), lambda i, j: (i, ga_blk + j)),
            pl.BlockSpec((bm, bn), lambda i, j: (i, gb_blk + j)),
        ],
        out_specs=pl.BlockSpec((bm, bn), lambda i, j: (i, j)),
        out_shape=jax.ShapeDtypeStruct((m, n), BF16),
        compiler_params=_cparams(("parallel", "parallel")),
        name="merge_up",
    )(o_a, w_a, o_b, w_b, proj, proj)


def _ple_body(h_ref, wg_ref, p_ref, wp_ref, x_ref, o_ref):
    gate = _sigmoid(_dot(h_ref[...], wg_ref[...]))
    o_ref[...] = x_ref[...] + gate * _dot(p_ref[...], wp_ref[...])


def _ple(hp, w_gate, p, w_proj, x, *, bm=1024, bn=512):
    m, kd = hp.shape
    kp = p.shape[1]
    n = w_gate.shape[1]
    return pl.pallas_call(
        _ple_body,
        grid=(m // bm, n // bn),
        in_specs=[
            pl.BlockSpec((bm, kd), lambda i, j: (i, 0)),
            pl.BlockSpec((kd, bn), lambda i, j: (0, j)),
            pl.BlockSpec((bm, kp), lambda i, j: (i, 0)),
            pl.BlockSpec((kp, bn), lambda i, j: (0, j)),
            pl.BlockSpec((bm, bn), lambda i, j: (i, j)),
        ],
        out_specs=pl.BlockSpec((bm, bn), lambda i, j: (i, j)),
        out_shape=jax.ShapeDtypeStruct((m, n), F32),
        compiler_params=_cparams(("parallel", "parallel")),
        name="ple",
    )(hp, w_gate, p, w_proj, x)


def _norm_rope(x, gain, cos, sin_signed):
    y = x * lax.rsqrt(jnp.mean(x * x, axis=-1, keepdims=True) + NORM_EPS) * gain
    return y * cos + pltpu.roll(y, HEAD_DIM // 2, 1) * sin_signed


def _nsa_prep_body(q_ref, ks_ref, vs_ref, kw_ref, vw_ref, cos_ref, sin_ref, qg_ref, ksg_ref, kwg_ref,
                   qo_ref, ksat_ref, vso_ref, kwt_ref, vwo_ref, *, rows, blocks_per_seq):
    cos = cos_ref[...]
    sin = sin_ref[...]
    q_scale = (HEAD_DIM ** -0.5) * LOG2_E
    for h in range(NSA_HEADS):
        sl = slice(h * HEAD_DIM, (h + 1) * HEAD_DIM)
        qo_ref[:, sl] = (_norm_rope(q_ref[:, sl], qg_ref[...], cos, sin) * q_scale).astype(BF16)
    t = (pl.program_id(0) % blocks_per_seq) * rows + _iota((LANES, rows), 1)
    onehot_t = (_iota((LANES, rows), 0) == t // SEL_BLOCK).astype(BF16)
    for g in range(NSA_KV_HEADS):
        sl = slice(g * HEAD_DIM, (g + 1) * HEAD_DIM)
        ksat_ref[g, 0:HEAD_DIM, :] = _norm_rope(ks_ref[:, sl], ksg_ref[...], cos, sin).T.astype(BF16)
        ksat_ref[g, HEAD_DIM:2 * HEAD_DIM, :] = onehot_t
        kwt_ref[g] = _norm_rope(kw_ref[:, sl], kwg_ref[...], cos, sin).T.astype(BF16)
    vso_ref[...] = vs_ref[...].astype(BF16)
    vwo_ref[...] = vw_ref[...].astype(BF16)


def _nsa_prep(proj, cos, sin_signed, q_gain, ks_gain, kw_gain, *, batch, seq, rows=256):
    m = proj.shape[0]
    bps = seq // rows
    kvw = NSA_KV_WIDTH
    kv_blk = OFF_NKV // kvw

    def col(c):
        return lambda i: (i, c)

    tab = pl.BlockSpec((rows, HEAD_DIM), lambda i: (i % bps, 0))
    gain = pl.BlockSpec((1, HEAD_DIM), lambda i: (0, 0))
    return pl.pallas_call(
        functools.partial(_nsa_prep_body, rows=rows, blocks_per_seq=bps),
        grid=(m // rows,),
        in_specs=[
            pl.BlockSpec((rows, NSA_WIDTH), col(OFF_NQ // NSA_WIDTH)),
            pl.BlockSpec((rows, kvw), col(kv_blk + 2)),
            pl.BlockSpec((rows, kvw), col(kv_blk + 3)),
            pl.BlockSpec((rows, kvw), col(kv_blk + 4)),
            pl.BlockSpec((rows, kvw), col(kv_blk + 5)),
            tab, tab, gain, gain, gain,
        ],
        out_specs=[
            pl.BlockSpec((rows, NSA_WIDTH), col(0)),
            pl.BlockSpec((None, NSA_KV_HEADS, 2 * HEAD_DIM, rows), lambda i: (i // bps, 0, 0, i % bps)),
            pl.BlockSpec((rows, kvw), col(0)),
            pl.BlockSpec((None, NSA_KV_HEADS, HEAD_DIM, rows), lambda i: (i // bps, 0, 0, i % bps)),
            pl.BlockSpec((rows, kvw), col(0)),
        ],
        out_shape=[
            jax.ShapeDtypeStruct((m, NSA_WIDTH), BF16),
            jax.ShapeDtypeStruct((batch, NSA_KV_HEADS, 2 * HEAD_DIM, seq), BF16),
            jax.ShapeDtypeStruct((m, kvw), BF16),
            jax.ShapeDtypeStruct((batch, NSA_KV_HEADS, HEAD_DIM, seq), BF16),
            jax.ShapeDtypeStruct((m, kvw), BF16),
        ],
        compiler_params=_cparams(("parallel",)),
        name="nsa_prep",
    )(proj, proj, proj, proj, proj, cos, sin_signed, q_gain.reshape(1, -1), ks_gain.reshape(1, -1), kw_gain.reshape(1, -1))


def _gelu_tanh(x):
    return 0.5 * x * (1.0 + jnp.tanh(np.sqrt(2.0 / np.pi).astype(np.float32) * (x + 0.044715 * (x * x * x))))


def _compress_one(x_ref, pe_ref, w1_ref, w2_ref, n_chunks):
    half = CMP_LEN // 2
    first = jnp.zeros((n_chunks, CMP_HIDDEN), F32)
    second = jnp.zeros((n_chunks, CMP_HIDDEN), F32)
    for l in range(half):
        xl = x_ref[pl.ds(l, n_chunks, stride=CMP_STRIDE), :]
        first += _dot((xl + pe_ref[l:l + 1, :]).astype(BF16), w1_ref[l])
        second += _dot((xl + pe_ref[half + l:half + l + 1, :]).astype(BF16), w1_ref[half + l])
    hid = _gelu_tanh(first + pltpu.roll(second, n_chunks - 1, 0))
    return _dot(hid.astype(BF16), w2_ref[...])


def _compress_body(xk_ref, xv_ref, pek_ref, pev_ref, wk1_ref, wk2_ref, wv1_ref, wv2_ref, kg_ref, cos_ref, sin_ref,
                   kto_ref, vo_ref, *, n_chunks):
    k = _compress_one(xk_ref, pek_ref, wk1_ref, wk2_ref, n_chunks)
    kto_ref[...] = _norm_rope(k, kg_ref[...], cos_ref[...], sin_ref[...]).T.astype(BF16)
    vo_ref[...] = _compress_one(xv_ref, pev_ref, wv1_ref, wv2_ref, n_chunks).astype(BF16)


def _compress(proj, pe_k, pe_v, wk1, wk2, wv1, wv2, kc_gain, cos_c, sin_c, *, batch, seq):
    n_chunks = seq // CMP_STRIDE
    kc_blk = OFF_NKV // HEAD_DIM
    vc_blk = (OFF_NKV + NSA_KV_WIDTH) // HEAD_DIM

    def full(shape):
        return pl.BlockSpec(shape, lambda b, g: (0,) * len(shape))

    return pl.pallas_call(
        functools.partial(_compress_body, n_chunks=n_chunks),
        grid=(batch, NSA_KV_HEADS),
        in_specs=[
            pl.BlockSpec((seq, HEAD_DIM), lambda b, g: (b, kc_blk + g)),
            pl.BlockSpec((seq, HEAD_DIM), lambda b, g: (b, vc_blk + g)),
            full((CMP_LEN, HEAD_DIM)), full((CMP_LEN, HEAD_DIM)),
            full((CMP_LEN, HEAD_DIM, CMP_HIDDEN)), full((CMP_HIDDEN, HEAD_DIM)),
            full((CMP_LEN, HEAD_DIM, CMP_HIDDEN)), full((CMP_HIDDEN, HEAD_DIM)),
            full((1, HEAD_DIM)), full((n_chunks, HEAD_DIM)), full((n_chunks, HEAD_DIM)),
        ],
        out_specs=[
            pl.BlockSpec((None, None, HEAD_DIM, n_chunks), lambda b, g: (b, g, 0, 0)),
            pl.BlockSpec((None, None, n_chunks, HEAD_DIM), lambda b, g: (b, g, 0, 0)),
        ],
        out_shape=[
            jax.ShapeDtypeStruct((batch, NSA_KV_HEADS, HEAD_DIM, n_chunks), BF16),
            jax.ShapeDtypeStruct((batch, NSA_KV_HEADS, n_chunks, HEAD_DIM), BF16),
        ],
        compiler_params=_cparams(("parallel", "parallel")),
        name="nsa_compress",
    )(proj, proj, pe_k, pe_v, wk1, wk2, wv1, wv2, kc_gain.reshape(1, -1), cos_c, sin_c)


def _softmax2(s, mask):
    s = jnp.where(mask, s, -MASK_BIG)
    e = jnp.where(mask, jnp.exp2(s - jnp.max(s, axis=-1, keepdims=True)), 0.0)
    return e / jnp.maximum(jnp.sum(e, axis=-1, keepdims=True), 1e-30)


def _select_body(q_ref, kct_ref, vc_ref, ov_ref, oc_ref, sb_ref, *, tq, top_k):
    t0 = pl.program_id(2) * tq
    n_cmp = kct_ref.shape[1]
    tpos = t0 + _iota((tq, 1), 0)
    mask = _iota((1, n_cmp), 1) * CMP_STRIDE + (CMP_LEN - 1) <= tpos
    kct = kct_ref[...]
    vc = vc_ref[...]
    heads = [slice(r * HEAD_DIM, (r + 1) * HEAD_DIM) for r in range(NSA_GROUP)]
    scores = [_dot(q_ref[:, sl], kct) for sl in heads]
    probs = [_softmax2(s, mask) for s in scores]
    for sl, p in zip(heads, probs):
        oc_ref[:, sl] = _dot(p.astype(BF16), vc)
    psum = functools.reduce(jnp.add, probs)
    ov = ov_ref[...]
    p_hi, p_mid, p_lo = _split3(psum)
    imp = _dot(p_hi, ov) + (_dot(p_mid, ov) + _dot(p_lo, ov))
    imp_t = imp.T
    n_lane = imp_t.shape[0]
    j = _iota((n_lane, tq), 0)
    cur = (t0 + _iota((n_lane, tq), 1)) // SEL_BLOCK
    valid = j <= cur
    forced = (j == 0) | (j == cur) | (j == cur - 1)
    score = jnp.where(forced, jnp.inf, jnp.where(valid, imp_t, -jnp.inf))
    sel = jnp.zeros((n_lane, tq), F32)
    for _ in range(top_k):
        best = jnp.max(score, axis=0, keepdims=True)
        first = jnp.min(jnp.where(score == best, j, n_lane), axis=0, keepdims=True)
        hit = j == first
        sel = jnp.where(hit, 1.0, sel)
        score = jnp.where(hit, -jnp.inf, score)
    sel = jnp.where(valid, sel, 0.0)
    sb_ref[...] = ((sel.T - 1.0) * BIAS_BIG).astype(BF16)


def _select(qn, k_cmp_t, v_cmp, overlap, *, batch, seq, tq=128):
    m = qn.shape[0]
    n_cmp = v_cmp.shape[2]
    gw = NSA_GROUP * HEAD_DIM
    nq = seq // tq
    top_k = min(SEL_TOPK, seq // SEL_BLOCK)
    return pl.pallas_call(
        functools.partial(_select_body, tq=tq, top_k=top_k),
        grid=(batch, NSA_KV_HEADS, nq),
        in_specs=[
            pl.BlockSpec((tq, gw), lambda b, g, i: (b * nq + i, g)),
            pl.BlockSpec((None, None, HEAD_DIM, n_cmp), lambda b, g, i: (b, g, 0, 0)),
            pl.BlockSpec((None, None, n_cmp, HEAD_DIM), lambda b, g, i: (b, g, 0, 0)),
            pl.BlockSpec((n_cmp, LANES), lambda b, g, i: (0, 0)),
        ],
        out_specs=[
            pl.BlockSpec((tq, gw), lambda b, g, i: (b * nq + i, g)),
            pl.BlockSpec((tq, LANES), lambda b, g, i: (b * nq + i, g)),
        ],
        out_shape=[
            jax.ShapeDtypeStruct((m, NSA_WIDTH), F32),
            jax.ShapeDtypeStruct((m, NSA_KV_HEADS * LANES), BF16),
        ],
        compiler_params=_cparams(("parallel", "parallel", "parallel")),
        name="nsa_select",
    )(qn, k_cmp_t, v_cmp, overlap)


def _lane_column(x, lane):
    return jnp.sum(jnp.where(_iota(x.shape, 1) == lane, x, 0.0), axis=1, keepdims=True)


def _flash_scores(s, m_ref, l_ref, r):
    chunks = [s[:, c * LANES:(c + 1) * LANES] for c in range(s.shape[1] // LANES)]
    m_prev = m_ref[r]
    row_max = jnp.max(functools.reduce(jnp.maximum, chunks), axis=-1, keepdims=True)
    m_new = jnp.maximum(m_prev, jnp.broadcast_to(row_max, m_prev.shape))
    alpha = jnp.exp2(m_prev - m_new)
    ps = [jnp.exp2(ch - m_new) for ch in chunks]
    l_ref[r] = alpha * l_ref[r] + functools.reduce(jnp.add, ps)
    m_ref[r] = m_new
    return jnp.concatenate([x.astype(BF16) for x in ps], axis=1), alpha


def _attend_body(q_ref, sb_ref, ksat_ref, vs_ref, kwt_ref, vw_ref, oc_ref, sm_ref, o_ref, m_scr, l_scr, acc_scr,
                 *, tq, tk):
    g = pl.program_id(1)
    t0 = pl.program_id(2) * tq
    tpos = t0 + _iota((tq, 1), 0)
    heads = range(NSA_GROUP)
    lanes = [slice(r * HEAD_DIM, (r + 1) * HEAD_DIM) for r in heads]

    sb = sb_ref[...]
    qa = [jnp.concatenate([q_ref[:, sl], sb], axis=1) for sl in lanes]
    m_scr[...] = jnp.full(m_scr.shape, -MASK_BIG, F32)
    l_scr[...] = jnp.zeros(l_scr.shape, F32)
    acc_scr[...] = jnp.zeros(acc_scr.shape, F32)

    def tile(kt, causal):
        k0 = pl.multiple_of(kt * tk, tk)
        k_tile = ksat_ref[:, pl.ds(k0, tk)]
        v_tile = vs_ref[pl.ds(k0, tk), :]
        scores = [_dot(qa[r], k_tile) for r in heads]
        if causal:
            keep = k0 + _iota((1, tk), 1) <= tpos
            scores = [jnp.where(keep, s, -MASK_BIG) for s in scores]
        stats = [_flash_scores(scores[r], m_scr, l_scr, r) for r in heads]
        for r, (p, alpha) in zip(heads, stats):
            acc_scr[r] = alpha * acc_scr[r] + _dot(p, v_tile)

    n_full = t0 // tk

    def body(kt, carry):
        tile(kt, False)
        return carry

    lax.fori_loop(0, n_full, body, 0)
    tile(n_full, True)

    band = WINDOW + tq
    s0 = pl.multiple_of(jnp.maximum(t0 - WINDOW, 0), tq)
    kw_band = kwt_ref[:, pl.ds(s0, band)]
    vw_band = vw_ref[pl.ds(s0, band), :]
    dist = tpos - (s0 + _iota((1, band), 1))
    in_window = (dist >= 0) & (dist < WINDOW)
    scores = [_dot(q_ref[:, sl], kw_band) for sl in lanes]
    probs = [_softmax2(s, in_window).astype(BF16) for s in scores]
    o_w = [_dot(p, vw_band) for p in probs]

    gates = _sigmoid(sm_ref[...])
    for r, sl in zip(heads, lanes):
        o_s = acc_scr[r] / jnp.maximum(jnp.sum(l_scr[r], axis=-1, keepdims=True), 1e-30)
        lane = SM_GATE + (g * NSA_GROUP + r) * 3
        out = (_lane_column(gates, lane) * oc_ref[:, sl] + _lane_column(gates, lane + 1) * o_s
               + _lane_column(gates, lane + 2) * o_w[r])
        o_ref[:, sl] = out.astype(BF16)


def _attend(qn, selbias, ksa_t, vs, kw_t, vw, o_c, small, *, batch, seq, tq=256, tk=512):
    m = qn.shape[0]
    gw = NSA_GROUP * HEAD_DIM
    nq = seq // tq
    tk = min(tk, seq)

    def tile_spec(width):
        return pl.BlockSpec((tq, width), lambda b, g, i: (b * nq + i, g))

    def seq_spec(width):
        return pl.BlockSpec((seq, width), lambda b, g, i: (b, g))

    def seq_t_spec(depth):
        return pl.BlockSpec((None, None, depth, seq), lambda b, g, i: (b, g, 0, 0))

    stat = pltpu.VMEM((NSA_GROUP, tq, LANES), F32)
    return pl.pallas_call(
        functools.partial(_attend_body, tq=tq, tk=tk),
        grid=(batch, NSA_KV_HEADS, nq),
        in_specs=[
            tile_spec(gw), tile_spec(LANES),
            seq_t_spec(2 * HEAD_DIM), seq_spec(HEAD_DIM), seq_t_spec(HEAD_DIM), seq_spec(HEAD_DIM),
            tile_spec(gw),
            pl.BlockSpec((tq, LANES), lambda b, g, i: (b * nq + i, 0)),
        ],
        out_specs=tile_spec(gw),
        out_shape=jax.ShapeDtypeStruct((m, NSA_WIDTH), BF16),
        scratch_shapes=[stat, stat, stat],
        compiler_params=_cparams(("parallel", "parallel", "arbitrary")),
        name="nsa_attend",
    )(qn, selbias, ksa_t, vs, kw_t, vw, o_c, small)


def _rope_tables(pos):
    half = HEAD_DIM // 2
    inv_freq = ROPE_THETA ** (-jnp.arange(half, dtype=F32) / half)
    ang = pos[:, None] * inv_freq[None, :]
    cos, sin = jnp.cos(ang), jnp.sin(ang)
    return jnp.concatenate([cos, cos], axis=-1), jnp.concatenate([-sin, sin], axis=-1)


def _overlap_matrix(n_cmp_pad, n_cmp):
    c_start = np.arange(n_cmp_pad) * CMP_STRIDE
    s_start = np.arange(LANES) * SEL_BLOCK
    ov = (c_start[:, None] < s_start[None, :] + SEL_BLOCK) & (c_start[:, None] + CMP_LEN > s_start[None, :])
    ov &= (np.arange(n_cmp_pad) < n_cmp)[:, None]
    return jnp.asarray(ov.astype(np.float32), dtype=BF16)


def _nsa(proj, small, q_gain, kc_gain, ks_gain, kw_gain, pe_k, pe_v, wk1, wk2, wv1, wv2, *, batch, seq):
    n_chunks = seq // CMP_STRIDE
    cos, sin = _rope_tables(jnp.arange(seq, dtype=F32))
    cmp_end = jnp.arange(n_chunks) * CMP_STRIDE + (CMP_LEN - 1)
    cos_c, sin_c = _rope_tables(cmp_end.astype(F32))
    qn, ksa_t, vs, kw_t, vw = _nsa_prep(proj, cos, sin, q_gain, ks_gain, kw_gain, batch=batch, seq=seq)
    k_cmp_t, v_cmp = _compress(proj, pe_k, pe_v, wk1.astype(BF16), wk2.astype(BF16), wv1.astype(BF16),
                               wv2.astype(BF16), kc_gain, cos_c, sin_c, batch=batch, seq=seq)
    o_c, selbias = _select(qn, k_cmp_t, v_cmp, _overlap_matrix(n_chunks, n_chunks - 1), batch=batch, seq=seq)
    return _attend(qn, selbias, ksa_t, vs, kw_t, vw, o_c, small, batch=batch, seq=seq)


def _unit_lower_inverses(mats):
    c = mats[0].shape[0]
    row = _iota((c, c), 0)
    col = _iota((c, c), 1)
    eye = (row == col).astype(F32)
    diag = row // INV_BASE == col // INV_BASE
    xs = [_split2(jnp.where(diag, -a, 0.0)) for a in mats]
    ts = [eye + jnp.where(diag, -a, 0.0) for a in mats]
    powers = [_dot_hp(x, x) for x in xs]
    span = 2
    while span < INV_BASE:
        pw = [_split2(p) for p in powers]
        if 2 * span < INV_BASE:
            both = [_dot_hp(p, _split2(jnp.concatenate([q, t], axis=1))) for p, q, t in zip(pw, powers, ts)]
            ts = [t + b[:, c:] for t, b in zip(ts, both)]
            powers = [b[:, :c] for b in both]
        else:
            ts = [t + _dot_hp(p, _split2(t)) for p, t in zip(pw, ts)]
        span *= 2
    size = INV_BASE
    while size < c:
        off = (row // (2 * size) == col // (2 * size)) & (row // size != col // size)
        tp = [_split2(t) for t in ts]
        inner = [_dot_hp(_split2(jnp.where(off, a, 0.0)), t) for a, t in zip(mats, tp)]
        ts = [t - _dot_hp(t2, _split2(i)) for t, t2, i in zip(ts, tp, inner)]
        size *= 2
    return ts


def _gdn_body(xq_ref, xk_ref, xv_ref, z_ref, sm_ref, wq_ref, wk_ref, wv_ref, alog_ref, dtb_ref, og_ref, o_ref,
              xs_scr, state_scr, *, heads):
    c = GDN_CHUNK
    hb = pl.program_id(1)
    step = pl.program_id(2)
    halo = 8
    hs = range(heads)

    @pl.when(step == 0)
    def _():
        xs_scr[:, 0:halo, :] = jnp.zeros((3, halo, xs_scr.shape[2]), F32)
        state_scr[...] = jnp.zeros_like(state_scr)

    conv = []
    for idx, (x_ref, w_ref) in enumerate(((xq_ref, wq_ref), (xk_ref, wk_ref), (xv_ref, wv_ref))):
        xs_scr[idx, halo:halo + c, :] = x_ref[...]
        y = jnp.zeros(x_ref.shape, F32)
        for tap in range(GDN_CONV):
            y = y + w_ref[tap:tap + 1, :] * xs_scr[idx, pl.ds(halo - (GDN_CONV - 1) + tap, c), :]
        xs_scr[idx, 0:halo, :] = xs_scr[idx, c:c + halo, :]
        conv.append(_silu(y))
    qc, kc, vc = conv

    sm = sm_ref[...]
    beta_all = _sigmoid(sm)
    x = sm + dtb_ref[...]
    softplus = jnp.maximum(x, 0.0) + jnp.log1p(jnp.exp(-jnp.abs(x)))
    g_all = -(jnp.exp(alog_ref[...]) * softplus)
    row = _iota((c, c), 0)
    col = _iota((c, c), 1)
    tril = row >= col
    ones_tril = tril.astype(BF16)
    g_hi, g_mid, g_lo = _split3(g_all)
    gc_all = _dot(ones_tril, g_hi) + (_dot(ones_tril, g_mid) + _dot(ones_tril, g_lo))
    gc_all_t = gc_all.T

    def head_lanes(h):
        return slice(h * HEAD_DIM, (h + 1) * HEAD_DIM)

    qs, ks, gammas, g_cols, g_lasts, betas = [], [], [], [], [], []
    for h in hs:
        q = qc[:, head_lanes(h)]
        k = kc[:, head_lanes(h)]
        qs.append(q * lax.rsqrt(jnp.sum(q * q, axis=-1, keepdims=True) + NORM_EPS) * (HEAD_DIM ** -0.5))
        ks.append(k * lax.rsqrt(jnp.sum(k * k, axis=-1, keepdims=True) + NORM_EPS))
        head = hb * heads + h
        betas.append(_lane_column(beta_all, SM_BETA + head))
        g_col = _lane_column(gc_all, SM_DECAY + head)
        g_row = jnp.sum(jnp.where(_iota((LANES, c), 0) == SM_DECAY + head, gc_all_t, 0.0), axis=0, keepdims=True)
        gammas.append(jnp.exp(jnp.where(tril, g_col - g_row, -MASK_BIG)))
        g_cols.append(g_col)
        g_lasts.append(g_row[:, c - 1:c])
    decays = [jnp.exp(g) for g in g_cols]
    kbs = [k * b for k, b in zip(ks, betas)]
    k16 = [k.astype(BF16) for k in ks]
    kk_qk = [_dot_nt(jnp.concatenate([kb, q], axis=0).astype(BF16), kk) for kb, q, kk in zip(kbs, qs, k16)]
    t_inv = _unit_lower_inverses([jnp.where(row > col, x[:c] * gm, 0.0) for x, gm in zip(kk_qk, gammas)])
    uw = [_dot(t.astype(BF16), jnp.concatenate([vc[:, head_lanes(h)] * b, kb * d], axis=1).astype(BF16))
          for h, t, b, kb, d in zip(hs, t_inv, betas, kbs, decays)]
    states = [state_scr[h] for h in hs]
    ws_qs = [_dot(jnp.concatenate([u[:, HEAD_DIM:], q * d], axis=0).astype(BF16), s.astype(BF16))
             for u, q, d, s in zip(uw, qs, decays, states)]
    v_new = [(u[:, :HEAD_DIM] - w[:c]).astype(BF16) for u, w in zip(uw, ws_qs)]
    outs = [w[c:] + _dot((x[c:] * gm).astype(BF16), v) for w, x, gm, v in zip(ws_qs, kk_qk, gammas, v_new)]
    for h in hs:
        k_tail = (ks[h] * jnp.exp(g_lasts[h] - g_cols[h])).astype(BF16)
        state_scr[h] = states[h] * jnp.exp(g_lasts[h]) + _dot_tn(k_tail, v_new[h])
    for h in hs:
        o = outs[h]
        o = o * lax.rsqrt(jnp.mean(o * o, axis=-1, keepdims=True) + NORM_EPS) * og_ref[...]
        o_ref[:, head_lanes(h)] = (o * _silu(z_ref[:, head_lanes(h)])).astype(BF16)


def _lane_row(values, offset):
    return jnp.zeros((1, LANES), F32).at[0, offset:offset + values.shape[0]].set(values)


def _gdn(proj, small, conv_w, a_log, dt_bias, o_gain, *, batch, seq):
    m = proj.shape[0]
    c = GDN_CHUNK
    heads = GDN_HEAD_BLOCK
    width = heads * HEAD_DIM
    n_steps = seq // c

    def act_spec(off):
        base = off // width
        return pl.BlockSpec((c, width), lambda b, hb, s: (b * n_steps + s, base + hb))

    def conv_spec(part):
        base = part * GDN_WIDTH // width
        return pl.BlockSpec((GDN_CONV, width), lambda b, hb, s: (0, base + hb))

    row_spec = pl.BlockSpec((1, LANES), lambda b, hb, s: (0, 0))
    return pl.pallas_call(
        functools.partial(_gdn_body, heads=heads),
        grid=(batch, GDN_HEADS // heads, n_steps),
        in_specs=[
            act_spec(OFF_GQKV), act_spec(OFF_GQKV + GDN_WIDTH), act_spec(OFF_GQKV + 2 * GDN_WIDTH), act_spec(OFF_GZ),
            pl.BlockSpec((c, LANES), lambda b, hb, s: (b * n_steps + s, 0)),
            conv_spec(0), conv_spec(1), conv_spec(2),
            row_spec, row_spec, row_spec,
        ],
        out_specs=pl.BlockSpec((c, width), lambda b, hb, s: (b * n_steps + s, hb)),
        out_shape=jax.ShapeDtypeStruct((m, GDN_WIDTH), BF16),
        scratch_shapes=[pltpu.VMEM((3, c + 8, width), F32), pltpu.VMEM((heads, HEAD_DIM, HEAD_DIM), F32)],
        compiler_params=_cparams(("parallel", "parallel", "arbitrary")),
        name="gdn",
    )(proj, proj, proj, proj, small, conv_w, conv_w, conv_w,
      _lane_row(a_log, SM_DECAY), _lane_row(dt_bias, SM_DECAY), o_gain.reshape(1, -1))


def _split_w_in(w_in):
    sizes = (NSA_WIDTH, 6 * NSA_KV_WIDTH, 3 * NSA_HEADS, 3 * GDN_WIDTH, GDN_WIDTH, GDN_HEADS, GDN_HEADS, 2 * D_MODEL)
    offs = np.cumsum((0,) + sizes)
    nq, nkv, ngate, gqkv, gz, gbeta, gdecay, merge = (w_in[:, offs[i]:offs[i + 1]] for i in range(len(sizes)))
    big = jnp.concatenate([merge, gqkv, nq, gz, nkv], axis=1).astype(BF16)
    pad = jnp.zeros((w_in.shape[0], LANES - (3 * NSA_HEADS + 2 * GDN_HEADS)), w_in.dtype)
    small = jnp.concatenate([ngate, gbeta, gdecay, pad], axis=1).astype(BF16)
    return big, small


def _relu2(acc):
    r = jnp.maximum(acc, 0.0)
    return r * r


def _add(acc, res):
    return res + acc


def _layer(x, p, g_mix, w_in, nsa_q_gain, nsa_kc_gain, nsa_ks_gain, nsa_kw_gain, cmp_pe_k, cmp_pe_v, cmp_wk1, cmp_wk2,
           cmp_wv1, cmp_wv2, gdn_conv_w, gdn_a_log, gdn_dt_bias, gdn_o_gain, w_up_nsa, w_up_gdn, w_out, g_mlp,
           w_mlp_in, w_mlp_out, g_ple, w_ple_gate, w_ple_proj, *, batch, seq):
    h = _rmsnorm(x, g_mix)
    w_big, w_small = _split_w_in(w_in)
    proj = _matmul(h, w_big, bm=1024, bn=512, out_dtype=F32, name="in_proj")
    small = _matmul(h, w_small, bm=1024, bn=LANES, out_dtype=F32, name="in_proj_small")
    o_a = _nsa(proj, small, nsa_q_gain, nsa_kc_gain, nsa_ks_gain, nsa_kw_gain, cmp_pe_k, cmp_pe_v, cmp_wk1, cmp_wk2,
               cmp_wv1, cmp_wv2, batch=batch, seq=seq)
    o_b = _gdn(proj, small, gdn_conv_w, gdn_a_log, gdn_dt_bias, gdn_o_gain, batch=batch, seq=seq)
    mixed = _merge(o_a, w_up_nsa.astype(BF16), o_b, w_up_gdn.astype(BF16), proj)
    x = _matmul(mixed, w_out.astype(BF16), bm=1024, bn=512, out_dtype=F32, epilogue=_add, extras=(x,), name="out_proj")
    hm = _rmsnorm(x, g_mlp)
    hidden = _matmul(hm, w_mlp_in.astype(BF16), bm=1024, bn=512, out_dtype=BF16, epilogue=_relu2, name="mlp_in")
    x = _matmul(hidden, w_mlp_out.astype(BF16), bm=1024, bn=1024, bk=2048, out_dtype=F32, epilogue=_add, extras=(x,),
                name="mlp_out")
    hp = _rmsnorm(x, g_ple)
    return _ple(hp, w_ple_gate.astype(BF16), p.astype(BF16), w_ple_proj.astype(BF16), x)


def kernel(x, p, g_mix, w_in, nsa_q_gain, nsa_kc_gain, nsa_ks_gain, nsa_kw_gain, cmp_pe_k, cmp_pe_v, cmp_wk1, cmp_wk2,
           cmp_wv1, cmp_wv2, gdn_conv_w, gdn_a_log, gdn_dt_bias, gdn_o_gain, w_up_nsa, w_up_gdn, w_out, g_mlp,
           w_mlp_in, w_mlp_out, g_ple, w_ple_gate, w_ple_proj):
    batch, seq, d = x.shape
    depth = p.shape[0]
    assert seq // SEL_BLOCK <= LANES and seq % (4 * LANES) == 0
    y = x.reshape(batch * seq, d)
    for i in range(depth):
        y = _layer(y, p[i].reshape(batch * seq, -1), g_mix[i], w_in[i], nsa_q_gain[i], nsa_kc_gain[i], nsa_ks_gain[i],
                   nsa_kw_gain[i], cmp_pe_k[i], cmp_pe_v[i], cmp_wk1[i], cmp_wk2[i], cmp_wv1[i], cmp_wv2[i],
                   gdn_conv_w[i], gdn_a_log[i], gdn_dt_bias[i], gdn_o_gain[i], w_up_nsa[i], w_up_gdn[i], w_out[i],
                   g_mlp[i], w_mlp_in[i], w_mlp_out[i], g_ple[i], w_ple_gate[i], w_ple_proj[i], batch=batch, seq=seq)
    return y.reshape(batch, seq, d)
```

```python
import functools

import numpy as np
import jax
import jax.numpy as jnp
from jax import lax
from jax.experimental import pallas as pl
from jax.experimental.pallas import tpu as pltpu

D_MODEL = 4096
HEAD_DIM = 128
NSA_HEADS = 16
NSA_KV_HEADS = 4
NSA_GROUP = NSA_HEADS // NSA_KV_HEADS
CMP_LEN = 32
CMP_STRIDE = 16
CMP_HIDDEN = 4 * HEAD_DIM
SEL_BLOCK = 64
SEL_TOPK = 16
WINDOW = 512
GDN_HEADS = 16
GDN_CONV = 4
MLP_HIDDEN = 4 * D_MODEL
PLE_DIM = 256
ROPE_THETA = 10000.0
NORM_EPS = 1e-6
MASK_BIG = 1e30
BIAS_BIG = 2.0 ** 99
LOG2_E = 1.4426950408889634

NSA_WIDTH = NSA_HEADS * HEAD_DIM
NSA_KV_WIDTH = NSA_KV_HEADS * HEAD_DIM
GDN_WIDTH = GDN_HEADS * HEAD_DIM
LANES = 128
GDN_CHUNK = 128
GDN_HEAD_BLOCK = 8
INV_BASE = 16

OFF_MERGE = 0
OFF_GQKV = OFF_MERGE + 2 * D_MODEL
OFF_NQ = OFF_GQKV + 3 * GDN_WIDTH
OFF_GZ = OFF_NQ + NSA_WIDTH
OFF_NKV = OFF_GZ + GDN_WIDTH
BIG_WIDTH = OFF_NKV + 6 * NSA_KV_WIDTH
SM_GATE = 0
SM_BETA = 3 * NSA_HEADS
SM_DECAY = SM_BETA + GDN_HEADS

V7X_VMEM_LIMIT = 56 * 1024 * 1024

DENSE_TILES = {
    "in_proj": dict(bm=1024, bn=1024),
    "in_proj_small": dict(bm=1024, bn=LANES),
    "merge_up": dict(bm=1024, bn=512),
    "out_proj": dict(bm=1024, bn=512),
    "mlp_in": dict(bm=1024, bn=1024),
    "mlp_out": dict(bm=1024, bn=1024, bk=2048),
    "ple": dict(bm=1024, bn=512),
}

F32 = jnp.float32
BF16 = jnp.bfloat16


def _cparams(sem, vmem=V7X_VMEM_LIMIT):
    return pltpu.CompilerParams(dimension_semantics=sem, vmem_limit_bytes=vmem)


def _dot(a, b):
    return jnp.dot(a, b, preferred_element_type=F32)


def _dot_nt(a, b):
    return lax.dot_general(a, b, (((1,), (1,)), ((), ())), preferred_element_type=F32)


def _dot_tn(a, b):
    return lax.dot_general(a, b, (((0,), (0,)), ((), ())), preferred_element_type=F32)


def _split3(x):
    hi = x.astype(BF16)
    r = x - hi.astype(F32)
    mid = r.astype(BF16)
    lo = (r - mid.astype(F32)).astype(BF16)
    return hi, mid, lo


def _split2(x):
    hi = x.astype(BF16)
    lo = (x - hi.astype(F32)).astype(BF16)
    return hi, lo


def _dot_hp(a, b):
    return _dot(a[0], b[0]) + (_dot(a[0], b[1]) + _dot(a[1], b[0]))


def _sigmoid(x):
    return 0.5 * jnp.tanh(0.5 * x) + 0.5


def _silu(x):
    h = 0.5 * x
    return h * jnp.tanh(h) + h


def _iota(shape, dim):
    return lax.broadcasted_iota(jnp.int32, shape, dim)


def _rmsnorm_body(x_ref, g_ref, o_ref):
    x = x_ref[...]
    y = x * lax.rsqrt(jnp.mean(x * x, axis=-1, keepdims=True) + NORM_EPS)
    o_ref[...] = (y * g_ref[...]).astype(o_ref.dtype)


def _rmsnorm(x, gain, *, rows=256):
    m, d = x.shape
    return pl.pallas_call(
        _rmsnorm_body,
        grid=(m // rows,),
        in_specs=[pl.BlockSpec((rows, d), lambda i: (i, 0)), pl.BlockSpec((1, d), lambda i: (0, 0))],
        out_specs=pl.BlockSpec((rows, d), lambda i: (i, 0)),
        out_shape=jax.ShapeDtypeStruct((m, d), BF16),
        compiler_params=_cparams(("parallel",)),
        name="rmsnorm",
    )(x, gain.reshape(1, d))


def _mm_body(*refs, nk, n_extra, epilogue):
    a_ref, b_ref = refs[0], refs[1]
    extra = refs[2:2 + n_extra]
    o_ref = refs[2 + n_extra]
    if nk == 1:
        acc = _dot(a_ref[...], b_ref[...])
        o_ref[...] = epilogue(acc, *[e[...] for e in extra]).astype(o_ref.dtype)
    else:
        acc_ref = refs[3 + n_extra]
        k = pl.program_id(2)

        @pl.when(k == 0)
        def _():
            acc_ref[...] = jnp.zeros_like(acc_ref)

        acc_ref[...] += _dot(a_ref[...], b_ref[...])

        @pl.when(k == nk - 1)
        def _():
            o_ref[...] = epilogue(acc_ref[...], *[e[...] for e in extra]).astype(o_ref.dtype)


def _matmul(a, b, *, bm, bn, bk=None, out_dtype, epilogue=lambda acc: acc, extras=(), name):
    m, kdim = a.shape
    n = b.shape[1]
    bk = kdim if bk is None else bk
    nk = kdim // bk
    in_specs = [pl.BlockSpec((bm, bk), lambda i, j, k: (i, k)), pl.BlockSpec((bk, bn), lambda i, j, k: (k, j))]
    in_specs += [pl.BlockSpec((bm, bn), lambda i, j, k: (i, j)) for _ in extras]
    scratch = [pltpu.VMEM((bm, bn), F32)] if nk > 1 else []
    return pl.pallas_call(
        functools.partial(_mm_body, nk=nk, n_extra=len(extras), epilogue=epilogue),
        grid=(m // bm, n // bn, nk),
        in_specs=in_specs,
        out_specs=pl.BlockSpec((bm, bn), lambda i, j, k: (i, j)),
        out_shape=jax.ShapeDtypeStruct((m, n), out_dtype),
        scratch_shapes=scratch,
        compiler_params=_cparams(("parallel", "parallel", "arbitrary")),
        name=name,
    )(a, b, *extras)


def _merge_body(oa_ref, wa_ref, ob_ref, wb_ref, ga_ref, gb_ref, o_ref):
    ya = _dot(oa_ref[...], wa_ref[...])
    yb = _dot(ob_ref[...], wb_ref[...])
    o_ref[...] = (_sigmoid(ga_ref[...]) * ya + _sigmoid(gb_ref[...]) * yb).astype(o_ref.dtype)


def _merge(o_a, w_a, o_b, w_b, proj, *, bm, bn):
    m, ka = o_a.shape
    kb = o_b.shape[1]
    n = w_a.shape[1]
    ga_blk = OFF_MERGE // bn
    gb_blk = (OFF_MERGE + D_MODEL) // bn
    return pl.pallas_call(
        _merge_body,
        grid=(m // bm, n // bn),
        in_specs=[
            pl.BlockSpec((bm, ka), lambda i, j: (i, 0)),
            pl.BlockSpec((ka, bn), lambda i, j: (0, j)),
            pl.BlockSpec((bm, kb), lambda i, j: (i, 0)),
            pl.BlockSpec((kb, bn), lambda i, j: (0, j)),
            pl.BlockSpec((bm, bn), lambda i, j: (i, ga_blk + j)),
            pl.BlockSpec((bm, bn), lambda i, j: (i, gb_blk + j)),
        ],
        out_specs=pl.BlockSpec((bm, bn), lambda i, j: (i, j)),
        out_shape=jax.ShapeDtypeStruct((m, n), BF16),
        compiler_params=_cparams(("parallel", "parallel")),
        name="merge_up",
    )(o_a, w_a, o_b, w_b, proj, proj)


def _ple_body(h_ref, wg_ref, p_ref, wp_ref, x_ref, o_ref):
    gate = _sigmoid(_dot(h_ref[...], wg_ref[...]))
    o_ref[...] = x_ref[...] + gate * _dot(p_ref[...], wp_ref[...])


def _ple(hp, w_gate, p, w_proj, x, *, bm, bn):
    m, kd = hp.shape
    kp = p.shape[1]
    n = w_gate.shape[1]
    return pl.pallas_call(
        _ple_body,
        grid=(m // bm, n // bn),
        in_specs=[
            pl.BlockSpec((bm, kd), lambda i, j: (i, 0)),
            pl.BlockSpec((kd, bn), lambda i, j: (0, j)),
            pl.BlockSpec((bm, kp), lambda i, j: (i, 0)),
            pl.BlockSpec((kp, bn), lambda i, j: (0, j)),
            pl.BlockSpec((bm, bn), lambda i, j: (i, j)),
        ],
        out_specs=pl.BlockSpec((bm, bn), lambda i, j: (i, j)),
        out_shape=jax.ShapeDtypeStruct((m, n), F32),
        compiler_params=_cparams(("parallel", "parallel")),
        name="ple",
    )(hp, w_gate, p, w_proj, x)


def _norm_rope(x, gain, cos, sin_signed):
    y = x * lax.rsqrt(jnp.mean(x * x, axis=-1, keepdims=True) + NORM_EPS) * gain
    return y * cos + pltpu.roll(y, HEAD_DIM // 2, 1) * sin_signed


def _nsa_prep_body(q_ref, ks_ref, vs_ref, kw_ref, vw_ref, cos_ref, sin_ref, qg_ref, ksg_ref, kwg_ref,
                   qo_ref, ksat_ref, vso_ref, kwt_ref, vwo_ref, *, rows, blocks_per_seq):
    cos = cos_ref[...]
    sin = sin_ref[...]
    q_scale = (HEAD_DIM ** -0.5) * LOG2_E
    for h in range(NSA_HEADS):
        sl = slice(h * HEAD_DIM, (h + 1) * HEAD_DIM)
        qo_ref[:, sl] = (_norm_rope(q_ref[:, sl], qg_ref[...], cos, sin) * q_scale).astype(BF16)
    t = (pl.program_id(0) % blocks_per_seq) * rows + _iota((LANES, rows), 1)
    onehot_t = (_iota((LANES, rows), 0) == t // SEL_BLOCK).astype(BF16)
    for g in range(NSA_KV_HEADS):
        sl = slice(g * HEAD_DIM, (g + 1) * HEAD_DIM)
        ksat_ref[g, 0:HEAD_DIM, :] = _norm_rope(ks_ref[:, sl], ksg_ref[...], cos, sin).T.astype(BF16)
        ksat_ref[g, HEAD_DIM:2 * HEAD_DIM, :] = onehot_t
        kwt_ref[g] = _norm_rope(kw_ref[:, sl], kwg_ref[...], cos, sin).T.astype(BF16)
    vso_ref[...] = vs_ref[...].astype(BF16)
    vwo_ref[...] = vw_ref[...].astype(BF16)


def _nsa_prep(proj, cos, sin_signed, q_gain, ks_gain, kw_gain, *, batch, seq, rows=256):
    m = proj.shape[0]
    bps = seq // rows
    kvw = NSA_KV_WIDTH
    kv_blk = OFF_NKV // kvw

    def col(c):
        return lambda i: (i, c)

    tab = pl.BlockSpec((rows, HEAD_DIM), lambda i: (i % bps, 0))
    gain = pl.BlockSpec((1, HEAD_DIM), lambda i: (0, 0))
    return pl.pallas_call(
        functools.partial(_nsa_prep_body, rows=rows, blocks_per_seq=bps),
        grid=(m // rows,),
        in_specs=[
            pl.BlockSpec((rows, NSA_WIDTH), col(OFF_NQ // NSA_WIDTH)),
            pl.BlockSpec((rows, kvw), col(kv_blk + 2)),
            pl.BlockSpec((rows, kvw), col(kv_blk + 3)),
            pl.BlockSpec((rows, kvw), col(kv_blk + 4)),
            pl.BlockSpec((rows, kvw), col(kv_blk + 5)),
            tab, tab, gain, gain, gain,
        ],
        out_specs=[
            pl.BlockSpec((rows, NSA_WIDTH), col(0)),
            pl.BlockSpec((None, NSA_KV_HEADS, 2 * HEAD_DIM, rows), lambda i: (i // bps, 0, 0, i % bps)),
            pl.BlockSpec((rows, kvw), col(0)),
            pl.BlockSpec((None, NSA_KV_HEADS, HEAD_DIM, rows), lambda i: (i // bps, 0, 0, i % bps)),
            pl.BlockSpec((rows, kvw), col(0)),
        ],
        out_shape=[
            jax.ShapeDtypeStruct((m, NSA_WIDTH), BF16),
            jax.ShapeDtypeStruct((batch, NSA_KV_HEADS, 2 * HEAD_DIM, seq), BF16),
            jax.ShapeDtypeStruct((m, kvw), BF16),
            jax.ShapeDtypeStruct((batch, NSA_KV_HEADS, HEAD_DIM, seq), BF16),
            jax.ShapeDtypeStruct((m, kvw), BF16),
        ],
        compiler_params=_cparams(("parallel",)),
        name="nsa_prep",
    )(proj, proj, proj, proj, proj, cos, sin_signed, q_gain.reshape(1, -1), ks_gain.reshape(1, -1), kw_gain.reshape(1, -1))


def _gelu_tanh(x):
    return 0.5 * x * (1.0 + jnp.tanh(np.sqrt(2.0 / np.pi).astype(np.float32) * (x + 0.044715 * (x * x * x))))


def _compress_one(x_ref, pe_ref, w1_ref, w2_ref, n_chunks):
    half = CMP_LEN // 2
    first = jnp.zeros((n_chunks, CMP_HIDDEN), F32)
    second = jnp.zeros((n_chunks, CMP_HIDDEN), F32)
    for l in range(half):
        xl = x_ref[pl.ds(l, n_chunks, stride=CMP_STRIDE), :]
        first += _dot((xl + pe_ref[l:l + 1, :]).astype(BF16), w1_ref[l])
        second += _dot((xl + pe_ref[half + l:half + l + 1, :]).astype(BF16), w1_ref[half + l])
    hid = _gelu_tanh(first + pltpu.roll(second, n_chunks - 1, 0))
    return _dot(hid.astype(BF16), w2_ref[...])


def _compress_body(xk_ref, xv_ref, pek_ref, pev_ref, wk1_ref, wk2_ref, wv1_ref, wv2_ref, kg_ref, cos_ref, sin_ref,
                   kto_ref, vo_ref, *, n_chunks):
    k = _compress_one(xk_ref, pek_ref, wk1_ref, wk2_ref, n_chunks)
    kto_ref[...] = _norm_rope(k, kg_ref[...], cos_ref[...], sin_ref[...]).T.astype(BF16)
    vo_ref[...] = _compress_one(xv_ref, pev_ref, wv1_ref, wv2_ref, n_chunks).astype(BF16)


def _compress(proj, pe_k, pe_v, wk1, wk2, wv1, wv2, kc_gain, cos_c, sin_c, *, batch, seq):
    n_chunks = seq // CMP_STRIDE
    kc_blk = OFF_NKV // HEAD_DIM
    vc_blk = (OFF_NKV + NSA_KV_WIDTH) // HEAD_DIM

    def full(shape):
        return pl.BlockSpec(shape, lambda b, g: (0,) * len(shape))

    return pl.pallas_call(
        functools.partial(_compress_body, n_chunks=n_chunks),
        grid=(batch, NSA_KV_HEADS),
        in_specs=[
            pl.BlockSpec((seq, HEAD_DIM), lambda b, g: (b, kc_blk + g)),
            pl.BlockSpec((seq, HEAD_DIM), lambda b, g: (b, vc_blk + g)),
            full((CMP_LEN, HEAD_DIM)), full((CMP_LEN, HEAD_DIM)),
            full((CMP_LEN, HEAD_DIM, CMP_HIDDEN)), full((CMP_HIDDEN, HEAD_DIM)),
            full((CMP_LEN, HEAD_DIM, CMP_HIDDEN)), full((CMP_HIDDEN, HEAD_DIM)),
            full((1, HEAD_DIM)), full((n_chunks, HEAD_DIM)), full((n_chunks, HEAD_DIM)),
        ],
        out_specs=[
            pl.BlockSpec((None, None, HEAD_DIM, n_chunks), lambda b, g: (b, g, 0, 0)),
            pl.BlockSpec((None, None, n_chunks, HEAD_DIM), lambda b, g: (b, g, 0, 0)),
        ],
        out_shape=[
            jax.ShapeDtypeStruct((batch, NSA_KV_HEADS, HEAD_DIM, n_chunks), BF16),
            jax.ShapeDtypeStruct((batch, NSA_KV_HEADS, n_chunks, HEAD_DIM), BF16),
        ],
        compiler_params=_cparams(("parallel", "parallel")),
        name="nsa_compress",
    )(proj, proj, pe_k, pe_v, wk1, wk2, wv1, wv2, kc_gain.reshape(1, -1), cos_c, sin_c)


def _softmax2(s, mask):
    s = jnp.where(mask, s, -MASK_BIG)
    e = jnp.where(mask, jnp.exp2(s - jnp.max(s, axis=-1, keepdims=True)), 0.0)
    return e * (1.0 / jnp.maximum(jnp.sum(e, axis=-1, keepdims=True), 1e-30))


def _select_blocks(imp, t0, top_k):
    tq = imp.shape[0]
    imp_t = imp.T
    n_lane = imp_t.shape[0]
    j = _iota((n_lane, tq), 0)
    cur = (t0 + _iota((n_lane, tq), 1)) // SEL_BLOCK
    valid = j <= cur
    forced = (j == 0) | (j == cur) | (j == cur - 1)
    score = jnp.where(valid & jnp.logical_not(forced), imp_t, -jnp.inf)
    for _ in range(top_k - 3):
        best = jnp.max(score, axis=0, keepdims=True)
        first = jnp.min(jnp.where(score == best, j, n_lane), axis=0, keepdims=True)
        score = jnp.where(j == first, -jnp.inf, score)
    sel = jnp.where(valid & (score == -jnp.inf), 0.0, -BIAS_BIG)
    return sel.T.astype(BF16)


def _lane_column(x, lane):
    return jnp.sum(jnp.where(_iota(x.shape, 1) == lane, x, 0.0), axis=1, keepdims=True)


def _flash_scores(s, m_ref, l_ref, r):
    chunks = [s[:, c * LANES:(c + 1) * LANES] for c in range(s.shape[1] // LANES)]
    m_prev = m_ref[r]
    row_max = jnp.max(functools.reduce(jnp.maximum, chunks), axis=-1, keepdims=True)
    m_new = jnp.maximum(m_prev, jnp.broadcast_to(row_max, m_prev.shape))
    alpha = jnp.exp2(m_prev - m_new)
    ps = [jnp.exp2(ch - m_new) for ch in chunks]
    l_ref[r] = alpha * l_ref[r] + functools.reduce(jnp.add, ps)
    m_ref[r] = m_new
    return jnp.concatenate([x.astype(BF16) for x in ps], axis=1), alpha


def _attend_body(q_ref, kct_ref, vc_ref, ov_ref, ksat_ref, vs_ref, kwt_ref, vw_ref, sm_ref, o_ref,
                 m_scr, l_scr, acc_scr, oc_scr, *, tq, tk, tail, top_k):
    g = pl.program_id(1)
    t0 = pl.program_id(2) * tq
    tpos = t0 + _iota((tq, 1), 0)
    heads = range(NSA_GROUP)
    lanes = [slice(r * HEAD_DIM, (r + 1) * HEAD_DIM) for r in heads]
    q = [q_ref[:, sl] for sl in lanes]

    n_cmp = kct_ref.shape[1]
    visible = _iota((1, n_cmp), 1) * CMP_STRIDE + (CMP_LEN - 1) <= tpos
    kct = kct_ref[...]
    vc = vc_ref[...]
    probs = [_softmax2(s, visible) for s in [_dot(x, kct) for x in q]]
    for r, p in zip(heads, probs):
        oc_scr[r] = _dot(p.astype(BF16), vc)
    ov = ov_ref[...]
    p_hi, p_mid, p_lo = _split3(functools.reduce(jnp.add, probs))
    imp = _dot(p_hi, ov) + (_dot(p_mid, ov) + _dot(p_lo, ov))
    sb = _select_blocks(imp, t0, top_k)

    qa = [jnp.concatenate([x, sb], axis=1) for x in q]
    m_scr[...] = jnp.full(m_scr.shape, -MASK_BIG, F32)
    l_scr[...] = jnp.zeros(l_scr.shape, F32)
    acc_scr[...] = jnp.zeros(acc_scr.shape, F32)

    def tile(k0, width, causal):
        k_tile = ksat_ref[:, pl.ds(k0, width)]
        v_tile = vs_ref[pl.ds(k0, width), :]
        scores = [_dot(x, k_tile) for x in qa]
        if causal:
            keep = k0 + _iota((1, width), 1) <= tpos
            scores = [jnp.where(keep, s, -MASK_BIG) for s in scores]
        stats = [_flash_scores(scores[r], m_scr, l_scr, r) for r in heads]
        for r, (p, alpha) in zip(heads, stats):
            acc_scr[r] = alpha * acc_scr[r] + _dot(p, v_tile)

    n_full = t0 // tk

    def full_body(kt, carry):
        tile(pl.multiple_of(kt * tk, tk), tk, False)
        return carry

    lax.fori_loop(0, n_full, full_body, 0)
    done = n_full * tk

    def tail_body(j, carry):
        tile(pl.multiple_of(done + j * tail, tail), tail, True)
        return carry

    lax.fori_loop(0, (t0 + tq - done + tail - 1) // tail, tail_body, 0)

    band = WINDOW + tq
    s0 = pl.multiple_of(jnp.maximum(t0 - WINDOW, 0), tq)
    kw_band = kwt_ref[:, pl.ds(s0, band)]
    vw_band = vw_ref[pl.ds(s0, band), :]
    dist = tpos - (s0 + _iota((1, band), 1))
    in_window = (dist >= 0) & (dist < WINDOW)
    probs = [_softmax2(s, in_window).astype(BF16) for s in [_dot(x, kw_band) for x in q]]
    o_w = [_dot(p, vw_band) for p in probs]

    gates = _sigmoid(sm_ref[...])
    for r, sl in zip(heads, lanes):
        o_s = acc_scr[r] * (1.0 / jnp.maximum(jnp.sum(l_scr[r], axis=-1, keepdims=True), 1e-30))
        lane = SM_GATE + (g * NSA_GROUP + r) * 3
        out = (_lane_column(gates, lane) * oc_scr[r] + _lane_column(gates, lane + 1) * o_s
               + _lane_column(gates, lane + 2) * o_w[r])
        o_ref[:, sl] = out.astype(BF16)


def _attend(qn, k_cmp_t, v_cmp, overlap, ksa_t, vs, kw_t, vw, small, *, batch, seq, tq=256, tk=1024, tail=512):
    m = qn.shape[0]
    gw = NSA_GROUP * HEAD_DIM
    nq = seq // tq
    tk = min(tk, seq)
    n_cmp = v_cmp.shape[2]
    top_k = min(SEL_TOPK, seq // SEL_BLOCK)

    def tile_spec(width):
        return pl.BlockSpec((tq, width), lambda b, g, i: (b * nq + i, g))

    def seq_spec(width):
        return pl.BlockSpec((seq, width), lambda b, g, i: (b, g))

    def group_spec(rows, cols):
        return pl.BlockSpec((None, None, rows, cols), lambda b, g, i: (b, g, 0, 0))

    stat = pltpu.VMEM((NSA_GROUP, tq, LANES), F32)
    return pl.pallas_call(
        functools.partial(_attend_body, tq=tq, tk=tk, tail=tail, top_k=top_k),
        grid=(batch, NSA_KV_HEADS, nq),
        in_specs=[
            tile_spec(gw),
            group_spec(HEAD_DIM, n_cmp), group_spec(n_cmp, HEAD_DIM),
            pl.BlockSpec((n_cmp, LANES), lambda b, g, i: (0, 0)),
            group_spec(2 * HEAD_DIM, seq), seq_spec(HEAD_DIM), group_spec(HEAD_DIM, seq), seq_spec(HEAD_DIM),
            pl.BlockSpec((tq, LANES), lambda b, g, i: (b * nq + i, 0)),
        ],
        out_specs=tile_spec(gw),
        out_shape=jax.ShapeDtypeStruct((m, NSA_WIDTH), BF16),
        scratch_shapes=[stat, stat, stat, stat],
        compiler_params=_cparams(("parallel", "parallel", "arbitrary")),
        name="nsa_attend",
    )(qn, k_cmp_t, v_cmp, overlap, ksa_t, vs, kw_t, vw, small)


def _rope_tables(pos):
    half = HEAD_DIM // 2
    inv_freq = ROPE_THETA ** (-jnp.arange(half, dtype=F32) / half)
    ang = pos[:, None] * inv_freq[None, :]
    cos, sin = jnp.cos(ang), jnp.sin(ang)
    return jnp.concatenate([cos, cos], axis=-1), jnp.concatenate([-sin, sin], axis=-1)


def _overlap_matrix(n_cmp_pad, n_cmp):
    c_start = np.arange(n_cmp_pad) * CMP_STRIDE
    s_start = np.arange(LANES) * SEL_BLOCK
    ov = (c_start[:, None] < s_start[None, :] + SEL_BLOCK) & (c_start[:, None] + CMP_LEN > s_start[None, :])
    ov &= (np.arange(n_cmp_pad) < n_cmp)[:, None]
    return jnp.asarray(ov.astype(np.float32), dtype=BF16)


def _nsa(proj, small, q_gain, kc_gain, ks_gain, kw_gain, pe_k, pe_v, wk1, wk2, wv1, wv2, *, batch, seq):
    n_chunks = seq // CMP_STRIDE
    cos, sin = _rope_tables(jnp.arange(seq, dtype=F32))
    cmp_end = jnp.arange(n_chunks) * CMP_STRIDE + (CMP_LEN - 1)
    cos_c, sin_c = _rope_tables(cmp_end.astype(F32))
    qn, ksa_t, vs, kw_t, vw = _nsa_prep(proj, cos, sin, q_gain, ks_gain, kw_gain, batch=batch, seq=seq)
    k_cmp_t, v_cmp = _compress(proj, pe_k, pe_v, wk1.astype(BF16), wk2.astype(BF16), wv1.astype(BF16),
                               wv2.astype(BF16), kc_gain, cos_c, sin_c, batch=batch, seq=seq)
    overlap = _overlap_matrix(n_chunks, n_chunks - 1)
    return _attend(qn, k_cmp_t, v_cmp, overlap, ksa_t, vs, kw_t, vw, small, batch=batch, seq=seq)


def _unit_lower_inverses(mats):
    c = mats[0].shape[0]
    row = _iota((c, c), 0)
    col = _iota((c, c), 1)
    eye = (row == col).astype(F32)
    diag = row // INV_BASE == col // INV_BASE
    xs = [jnp.where(diag, -a, 0.0) for a in mats]
    ts = [eye + x for x in xs]
    x16 = [x.astype(BF16) for x in xs]
    powers = [_dot(x, x) for x in x16]
    span = 2
    while span < INV_BASE:
        p16 = [p.astype(BF16) for p in powers]
        if 2 * span < INV_BASE:
            both = [_dot(p, jnp.concatenate([p, t.astype(BF16)], axis=1)) for p, t in zip(p16, ts)]
            ts = [t + b[:, c:] for t, b in zip(ts, both)]
            powers = [b[:, :c] for b in both]
        else:
            ts = [t + _dot(p, t.astype(BF16)) for p, t in zip(p16, ts)]
        span *= 2
    size = INV_BASE
    while size < c:
        off = (row // (2 * size) == col // (2 * size)) & (row // size != col // size)
        t16 = [t.astype(BF16) for t in ts]
        inner = [_dot(jnp.where(off, a, 0.0).astype(BF16), t) for a, t in zip(mats, t16)]
        ts = [t - _dot(t2, i.astype(BF16)) for t, t2, i in zip(ts, t16, inner)]
        size *= 2
    resid = [eye - _dot_hp(_split2(eye + a), _split2(t)) for a, t in zip(mats, ts)]
    return [t + _dot(t.astype(BF16), r.astype(BF16)) for t, r in zip(ts, resid)]


def _gdn_body(xq_ref, xk_ref, xv_ref, z_ref, sm_ref, wq_ref, wk_ref, wv_ref, alog_ref, dtb_ref, og_ref, o_ref,
              xs_scr, state_scr, *, heads):
    c = GDN_CHUNK
    hb = pl.program_id(1)
    step = pl.program_id(2)
    halo = 8
    hs = range(heads)

    @pl.when(step == 0)
    def _():
        xs_scr[:, 0:halo, :] = jnp.zeros((3, halo, xs_scr.shape[2]), F32)
        state_scr[...] = jnp.zeros_like(state_scr)

    conv = []
    for idx, (x_ref, w_ref) in enumerate(((xq_ref, wq_ref), (xk_ref, wk_ref), (xv_ref, wv_ref))):
        xs_scr[idx, halo:halo + c, :] = x_ref[...]
        y = jnp.zeros(x_ref.shape, F32)
        for tap in range(GDN_CONV):
            y = y + w_ref[tap:tap + 1, :] * xs_scr[idx, pl.ds(halo - (GDN_CONV - 1) + tap, c), :]
        xs_scr[idx, 0:halo, :] = xs_scr[idx, c:c + halo, :]
        conv.append(_silu(y))
    qc, kc, vc = conv

    sm = sm_ref[...]
    beta_all = _sigmoid(sm)
    x = sm + dtb_ref[...]
    softplus = jnp.maximum(x, 0.0) + jnp.log1p(jnp.exp(-jnp.abs(x)))
    g_all = -(jnp.exp(alog_ref[...]) * softplus)
    row = _iota((c, c), 0)
    col = _iota((c, c), 1)
    tril = row >= col
    ones_tril = tril.astype(BF16)
    g_hi, g_mid, g_lo = _split3(g_all)
    gc_all = _dot(ones_tril, g_hi) + (_dot(ones_tril, g_mid) + _dot(ones_tril, g_lo))
    gc_all_t = gc_all.T

    def head_lanes(h):
        return slice(h * HEAD_DIM, (h + 1) * HEAD_DIM)

    qs, ks, gammas, g_cols, g_lasts, betas = [], [], [], [], [], []
    for h in hs:
        q = qc[:, head_lanes(h)]
        k = kc[:, head_lanes(h)]
        qs.append(q * lax.rsqrt(jnp.sum(q * q, axis=-1, keepdims=True) + NORM_EPS) * (HEAD_DIM ** -0.5))
        ks.append(k * lax.rsqrt(jnp.sum(k * k, axis=-1, keepdims=True) + NORM_EPS))
        head = hb * heads + h
        betas.append(_lane_column(beta_all, SM_BETA + head))
        g_col = _lane_column(gc_all, SM_DECAY + head)
        g_row = jnp.sum(jnp.where(_iota((LANES, c), 0) == SM_DECAY + head, gc_all_t, 0.0), axis=0, keepdims=True)
        gammas.append(jnp.exp(jnp.where(tril, g_col - g_row, -MASK_BIG)))
        g_cols.append(g_col)
        g_lasts.append(g_row[:, c - 1:c])
    decays = [jnp.exp(g) for g in g_cols]
    kbs = [k * b for k, b in zip(ks, betas)]
    k16 = [k.astype(BF16) for k in ks]
    kk_qk = [_dot_nt(jnp.concatenate([kb, q], axis=0).astype(BF16), kk) for kb, q, kk in zip(kbs, qs, k16)]
    t_inv = _unit_lower_inverses([jnp.where(row > col, x[:c] * gm, 0.0) for x, gm in zip(kk_qk, gammas)])
    uw = [_dot(t.astype(BF16), jnp.concatenate([vc[:, head_lanes(h)] * b, kb * d], axis=1).astype(BF16))
          for h, t, b, kb, d in zip(hs, t_inv, betas, kbs, decays)]
    states = [state_scr[h] for h in hs]
    ws_qs = [_dot(jnp.concatenate([u[:, HEAD_DIM:], q * d], axis=0).astype(BF16), s.astype(BF16))
             for u, q, d, s in zip(uw, qs, decays, states)]
    v_new = [(u[:, :HEAD_DIM] - w[:c]).astype(BF16) for u, w in zip(uw, ws_qs)]
    outs = [w[c:] + _dot((x[c:] * gm).astype(BF16), v) for w, x, gm, v in zip(ws_qs, kk_qk, gammas, v_new)]
    for h in hs:
        k_tail = (ks[h] * jnp.exp(g_lasts[h] - g_cols[h])).astype(BF16)
        state_scr[h] = states[h] * jnp.exp(g_lasts[h]) + _dot_tn(k_tail, v_new[h])
    for h in hs:
        o = outs[h]
        o = o * lax.rsqrt(jnp.mean(o * o, axis=-1, keepdims=True) + NORM_EPS) * og_ref[...]
        o_ref[:, head_lanes(h)] = (o * _silu(z_ref[:, head_lanes(h)])).astype(BF16)


def _lane_row(values, offset):
    return jnp.zeros((1, LANES), F32).at[0, offset:offset + values.shape[0]].set(values)


def _gdn(proj, small, conv_w, a_log, dt_bias, o_gain, *, batch, seq):
    m = proj.shape[0]
    c = GDN_CHUNK
    heads = GDN_HEAD_BLOCK
    width = heads * HEAD_DIM
    n_steps = seq // c

    def act_spec(off):
        base = off // width
        return pl.BlockSpec((c, width), lambda b, hb, s: (b * n_steps + s, base + hb))

    def conv_spec(part):
        base = part * GDN_WIDTH // width
        return pl.BlockSpec((GDN_CONV, width), lambda b, hb, s: (0, base + hb))

    row_spec = pl.BlockSpec((1, LANES), lambda b, hb, s: (0, 0))
    return pl.pallas_call(
        functools.partial(_gdn_body, heads=heads),
        grid=(batch, GDN_HEADS // heads, n_steps),
        in_specs=[
            act_spec(OFF_GQKV), act_spec(OFF_GQKV + GDN_WIDTH), act_spec(OFF_GQKV + 2 * GDN_WIDTH), act_spec(OFF_GZ),
            pl.BlockSpec((c, LANES), lambda b, hb, s: (b * n_steps + s, 0)),
            conv_spec(0), conv_spec(1), conv_spec(2),
            row_spec, row_spec, row_spec,
        ],
        out_specs=pl.BlockSpec((c, width), lambda b, hb, s: (b * n_steps + s, hb)),
        out_shape=jax.ShapeDtypeStruct((m, GDN_WIDTH), BF16),
        scratch_shapes=[pltpu.VMEM((3, c + 8, width), F32), pltpu.VMEM((heads, HEAD_DIM, HEAD_DIM), F32)],
        compiler_params=_cparams(("parallel", "parallel", "arbitrary")),
        name="gdn",
    )(proj, proj, proj, proj, small, conv_w, conv_w, conv_w,
      _lane_row(a_log, SM_DECAY), _lane_row(dt_bias, SM_DECAY), o_gain.reshape(1, -1))


def _split_w_in(w_in):
    sizes = (NSA_WIDTH, 6 * NSA_KV_WIDTH, 3 * NSA_HEADS, 3 * GDN_WIDTH, GDN_WIDTH, GDN_HEADS, GDN_HEADS, 2 * D_MODEL)
    offs = np.cumsum((0,) + sizes)
    w16 = w_in.astype(BF16)
    nq, nkv, ngate, gqkv, gz, gbeta, gdecay, merge = (w16[:, offs[i]:offs[i + 1]] for i in range(len(sizes)))
    big = jnp.concatenate([merge, gqkv, nq, gz, nkv], axis=1)
    pad = jnp.zeros((w_in.shape[0], LANES - (3 * NSA_HEADS + 2 * GDN_HEADS)), BF16)
    small = jnp.concatenate([ngate, gbeta, gdecay, pad], axis=1)
    return big, small


def _relu2(acc):
    r = jnp.maximum(acc, 0.0)
    return r * r


def _add(acc, res):
    return res + acc


def _layer(x, p, g_mix, w_in, nsa_q_gain, nsa_kc_gain, nsa_ks_gain, nsa_kw_gain, cmp_pe_k, cmp_pe_v, cmp_wk1, cmp_wk2,
           cmp_wv1, cmp_wv2, gdn_conv_w, gdn_a_log, gdn_dt_bias, gdn_o_gain, w_up_nsa, w_up_gdn, w_out, g_mlp,
           w_mlp_in, w_mlp_out, g_ple, w_ple_gate, w_ple_proj, *, batch, seq):
    h = _rmsnorm(x, g_mix)
    w_big, w_small = _split_w_in(w_in)
    proj = _matmul(h, w_big, out_dtype=F32, name="in_proj", **DENSE_TILES["in_proj"])
    small = _matmul(h, w_small, out_dtype=F32, name="in_proj_small", **DENSE_TILES["in_proj_small"])
    o_a = _nsa(proj, small, nsa_q_gain, nsa_kc_gain, nsa_ks_gain, nsa_kw_gain, cmp_pe_k, cmp_pe_v, cmp_wk1, cmp_wk2,
               cmp_wv1, cmp_wv2, batch=batch, seq=seq)
    o_b = _gdn(proj, small, gdn_conv_w, gdn_a_log, gdn_dt_bias, gdn_o_gain, batch=batch, seq=seq)
    mixed = _merge(o_a, w_up_nsa.astype(BF16), o_b, w_up_gdn.astype(BF16), proj, **DENSE_TILES["merge_up"])
    x = _matmul(mixed, w_out.astype(BF16), out_dtype=F32, epilogue=_add, extras=(x,), name="out_proj",
                **DENSE_TILES["out_proj"])
    hm = _rmsnorm(x, g_mlp)
    hidden = _matmul(hm, w_mlp_in.astype(BF16), out_dtype=BF16, epilogue=_relu2, name="mlp_in",
                     **DENSE_TILES["mlp_in"])
    x = _matmul(hidden, w_mlp_out.astype(BF16), out_dtype=F32, epilogue=_add, extras=(x,), name="mlp_out",
                **DENSE_TILES["mlp_out"])
    hp = _rmsnorm(x, g_ple)
    return _ple(hp, w_ple_gate.astype(BF16), p.astype(BF16), w_ple_proj.astype(BF16), x, **DENSE_TILES["ple"])


def kernel(x, p, g_mix, w_in, nsa_q_gain, nsa_kc_gain, nsa_ks_gain, nsa_kw_gain, cmp_pe_k, cmp_pe_v, cmp_wk1, cmp_wk2,
           cmp_wv1, cmp_wv2, gdn_conv_w, gdn_a_log, gdn_dt_bias, gdn_o_gain, w_up_nsa, w_up_gdn, w_out, g_mlp,
           w_mlp_in, w_mlp_out, g_ple, w_ple_gate, w_ple_proj):
    batch, seq, d = x.shape
    depth = p.shape[0]
    assert seq // SEL_BLOCK <= LANES and seq % (4 * LANES) == 0
    y = x.reshape(batch * seq, d)
    for i in range(depth):
        y = _layer(y, p[i].reshape(batch * seq, -1), g_mix[i], w_in[i], nsa_q_gain[i], nsa_kc_gain[i], nsa_ks_gain[i],
                   nsa_kw_gain[i], cmp_pe_k[i], cmp_pe_v[i], cmp_wk1[i], cmp_wk2[i], cmp_wv1[i], cmp_wv2[i],
                   gdn_conv_w[i], gdn_a_log[i], gdn_dt_bias[i], gdn_o_gain[i], w_up_nsa[i], w_up_gdn[i], w_out[i],
                   g_mlp[i], w_mlp_in[i], w_mlp_out[i], g_ple[i], w_ple_gate[i], w_ple_proj[i], batch=batch, seq=seq)
    return y.reshape(batch, seq, d)
```

```python
import functools

import numpy as np
import jax
import jax.numpy as jnp
from jax import lax
from jax.experimental import pallas as pl
from jax.experimental.pallas import tpu as pltpu

D_MODEL = 4096
HEAD_DIM = 128
NSA_HEADS = 16
NSA_KV_HEADS = 4
NSA_GROUP = NSA_HEADS // NSA_KV_HEADS
CMP_LEN = 32
CMP_STRIDE = 16
CMP_HIDDEN = 4 * HEAD_DIM
SEL_BLOCK = 64
SEL_TOPK = 16
WINDOW = 512
GDN_HEADS = 16
GDN_CONV = 4
MLP_HIDDEN = 4 * D_MODEL
PLE_DIM = 256
ROPE_THETA = 10000.0
NORM_EPS = 1e-6
MASK_BIG = 1e30
BIAS_BIG = 2.0 ** 99
LOG2_E = 1.4426950408889634

NSA_WIDTH = NSA_HEADS * HEAD_DIM
NSA_KV_WIDTH = NSA_KV_HEADS * HEAD_DIM
GDN_WIDTH = GDN_HEADS * HEAD_DIM
LANES = 128
GDN_CHUNK = 128
GDN_HEAD_BLOCK = 16
INV_BASE = 16

OFF_MERGE = 0
OFF_GQKV = OFF_MERGE + 2 * D_MODEL
OFF_NQ = OFF_GQKV + 3 * GDN_WIDTH
OFF_GZ = OFF_NQ + NSA_WIDTH
OFF_NKV = OFF_GZ + GDN_WIDTH
BIG_WIDTH = OFF_NKV + 6 * NSA_KV_WIDTH
SM_GATE = 0
SM_BETA = 3 * NSA_HEADS
SM_DECAY = SM_BETA + GDN_HEADS

V7X_VMEM_LIMIT = 56 * 1024 * 1024

DENSE_TILES = {
    "in_proj": dict(bm=1024, bn=1024),
    "in_proj_small": dict(bm=1024, bn=LANES),
    "merge_up": dict(bm=1024, bn=512),
    "out_proj": dict(bm=1024, bn=512),
    "mlp_in": dict(bm=1024, bn=1024),
    "mlp_out": dict(bm=1024, bn=1024, bk=2048),
    "ple": dict(bm=1024, bn=512),
}

F32 = jnp.float32
BF16 = jnp.bfloat16


def _cparams(sem, vmem=V7X_VMEM_LIMIT):
    return pltpu.CompilerParams(dimension_semantics=sem, vmem_limit_bytes=vmem)


def _dot(a, b):
    return jnp.dot(a, b, preferred_element_type=F32)


def _dot_nt(a, b):
    return lax.dot_general(a, b, (((1,), (1,)), ((), ())), preferred_element_type=F32)


def _dot_tn(a, b):
    return lax.dot_general(a, b, (((0,), (0,)), ((), ())), preferred_element_type=F32)


def _split3(x):
    hi = x.astype(BF16)
    r = x - hi.astype(F32)
    mid = r.astype(BF16)
    lo = (r - mid.astype(F32)).astype(BF16)
    return hi, mid, lo


def _split2(x):
    hi = x.astype(BF16)
    lo = (x - hi.astype(F32)).astype(BF16)
    return hi, lo


def _dot_hp(a, b):
    return _dot(a[0], b[0]) + (_dot(a[0], b[1]) + _dot(a[1], b[0]))


def _sigmoid(x):
    return 0.5 * jnp.tanh(0.5 * x) + 0.5


def _silu(x):
    h = 0.5 * x
    return h * jnp.tanh(h) + h


def _iota(shape, dim):
    return lax.broadcasted_iota(jnp.int32, shape, dim)


def _rmsnorm_body(x_ref, g_ref, o_ref):
    x = x_ref[...]
    y = x * lax.rsqrt(jnp.mean(x * x, axis=-1, keepdims=True) + NORM_EPS)
    o_ref[...] = (y * g_ref[...]).astype(o_ref.dtype)


def _rmsnorm(x, gain, *, rows=256):
    m, d = x.shape
    return pl.pallas_call(
        _rmsnorm_body,
        grid=(m // rows,),
        in_specs=[pl.BlockSpec((rows, d), lambda i: (i, 0)), pl.BlockSpec((1, d), lambda i: (0, 0))],
        out_specs=pl.BlockSpec((rows, d), lambda i: (i, 0)),
        out_shape=jax.ShapeDtypeStruct((m, d), BF16),
        compiler_params=_cparams(("parallel",)),
        name="rmsnorm",
    )(x, gain.reshape(1, d))


def _mm_body(a_ref, b_ref, o_ref):
    o_ref[...] = _dot(a_ref[...], b_ref[...]).astype(o_ref.dtype)


def _matmul(a, b, *, bm, bn, out_dtype, name):
    m, kdim = a.shape
    n = b.shape[1]
    return pl.pallas_call(
        _mm_body,
        grid=(m // bm, n // bn),
        in_specs=[pl.BlockSpec((bm, kdim), lambda i, j: (i, 0)), pl.BlockSpec((kdim, bn), lambda i, j: (0, j))],
        out_specs=pl.BlockSpec((bm, bn), lambda i, j: (i, j)),
        out_shape=jax.ShapeDtypeStruct((m, n), out_dtype),
        compiler_params=_cparams(("parallel", "parallel")),
        name=name,
    )(a, b)


def _norm_outputs(x, g_ref, xg_ref, ss_ref):
    xg_ref[...] = (x * g_ref[...]).astype(BF16)
    sq = x * x
    ss_ref[...] = functools.reduce(jnp.add, [sq[:, c * LANES:(c + 1) * LANES] for c in range(sq.shape[1] // LANES)])


def _row_scale(ss_ref, width):
    return lax.rsqrt(jnp.sum(ss_ref[...], axis=-1, keepdims=True) * (1.0 / width) + NORM_EPS)


def _norm_out_shapes(m, n, bn):
    return [jax.ShapeDtypeStruct((m, n), F32), jax.ShapeDtypeStruct((m, n), BF16),
            jax.ShapeDtypeStruct((m, n // bn * LANES), F32)]


def _out_proj_body(a_ref, b_ref, x_ref, g_ref, o_ref, xg_ref, ss_ref):
    x = x_ref[...] + _dot(a_ref[...], b_ref[...])
    o_ref[...] = x
    _norm_outputs(x, g_ref, xg_ref, ss_ref)


def _out_proj(a, b, x, gain, *, bm, bn):
    m, kdim = a.shape
    n = b.shape[1]
    tile = pl.BlockSpec((bm, bn), lambda i, j: (i, j))
    return pl.pallas_call(
        _out_proj_body,
        grid=(m // bm, n // bn),
        in_specs=[pl.BlockSpec((bm, kdim), lambda i, j: (i, 0)), pl.BlockSpec((kdim, bn), lambda i, j: (0, j)),
                  tile, pl.BlockSpec((1, bn), lambda i, j: (0, j))],
        out_specs=[tile, tile, pl.BlockSpec((bm, LANES), lambda i, j: (i, j))],
        out_shape=_norm_out_shapes(m, n, bn),
        compiler_params=_cparams(("parallel", "parallel")),
        name="out_proj",
    )(a, b, x, gain.reshape(1, n))


def _mlp_in_body(a_ref, b_ref, ss_ref, o_ref, *, width):
    r = jnp.maximum(_dot(a_ref[...], b_ref[...]) * _row_scale(ss_ref, width), 0.0)
    o_ref[...] = (r * r).astype(o_ref.dtype)


def _mlp_in(xg, b, ss, *, bm, bn):
    m, kdim = xg.shape
    n = b.shape[1]
    return pl.pallas_call(
        functools.partial(_mlp_in_body, width=kdim),
        grid=(m // bm, n // bn),
        in_specs=[pl.BlockSpec((bm, kdim), lambda i, j: (i, 0)), pl.BlockSpec((kdim, bn), lambda i, j: (0, j)),
                  pl.BlockSpec((bm, ss.shape[1]), lambda i, j: (i, 0))],
        out_specs=pl.BlockSpec((bm, bn), lambda i, j: (i, j)),
        out_shape=jax.ShapeDtypeStruct((m, n), BF16),
        compiler_params=_cparams(("parallel", "parallel")),
        name="mlp_in",
    )(xg, b, ss)


def _mlp_out_body(a_ref, b_ref, res_ref, g_ref, o_ref, xg_ref, ss_ref, *, nk):
    k = pl.program_id(2)

    @pl.when(k == 0)
    def _():
        o_ref[...] = res_ref[...]

    o_ref[...] += _dot(a_ref[...], b_ref[...])

    @pl.when(k == nk - 1)
    def _():
        _norm_outputs(o_ref[...], g_ref, xg_ref, ss_ref)


def _mlp_out(a, b, res, gain, *, bm, bn, bk):
    m, kdim = a.shape
    n = b.shape[1]
    nk = kdim // bk
    tile = pl.BlockSpec((bm, bn), lambda i, j, k: (i, j))
    return pl.pallas_call(
        functools.partial(_mlp_out_body, nk=nk),
        grid=(m // bm, n // bn, nk),
        in_specs=[pl.BlockSpec((bm, bk), lambda i, j, k: (i, k)), pl.BlockSpec((bk, bn), lambda i, j, k: (k, j)),
                  tile, pl.BlockSpec((1, bn), lambda i, j, k: (0, j))],
        out_specs=[tile, tile, pl.BlockSpec((bm, LANES), lambda i, j, k: (i, j))],
        out_shape=_norm_out_shapes(m, n, bn),
        compiler_params=_cparams(("parallel", "parallel", "arbitrary")),
        name="mlp_out",
    )(a, b, res, gain.reshape(1, n))


def _merge_body(oa_ref, wa_ref, ob_ref, wb_ref, ga_ref, gb_ref, o_ref):
    ya = _dot(oa_ref[...], wa_ref[...])
    yb = _dot(ob_ref[...], wb_ref[...])
    o_ref[...] = (_sigmoid(ga_ref[...]) * ya + _sigmoid(gb_ref[...]) * yb).astype(o_ref.dtype)


def _merge(o_a, w_a, o_b, w_b, proj, *, bm, bn):
    m, ka = o_a.shape
    kb = o_b.shape[1]
    n = w_a.shape[1]
    ga_blk = OFF_MERGE // bn
    gb_blk = (OFF_MERGE + D_MODEL) // bn
    return pl.pallas_call(
        _merge_body,
        grid=(m // bm, n // bn),
        in_specs=[
            pl.BlockSpec((bm, ka), lambda i, j: (i, 0)),
            pl.BlockSpec((ka, bn), lambda i, j: (0, j)),
            pl.BlockSpec((bm, kb), lambda i, j: (i, 0)),
            pl.BlockSpec((kb, bn), lambda i, j: (0, j)),
            pl.BlockSpec((bm, bn), lambda i, j: (i, ga_blk + j)),
            pl.BlockSpec((bm, bn), lambda i, j: (i, gb_blk + j)),
        ],
        out_specs=pl.BlockSpec((bm, bn), lambda i, j: (i, j)),
        out_shape=jax.ShapeDtypeStruct((m, n), BF16),
        compiler_params=_cparams(("parallel", "parallel")),
        name="merge_up",
    )(o_a, w_a, o_b, w_b, proj, proj)


def _ple_body(xg_ref, wg_ref, p_ref, wp_ref, x_ref, ss_ref, o_ref, *, width):
    gate = _sigmoid(_dot(xg_ref[...], wg_ref[...]) * _row_scale(ss_ref, width))
    o_ref[...] = x_ref[...] + gate * _dot(p_ref[...], wp_ref[...])


def _ple(xg, w_gate, p, w_proj, x, ss, *, bm, bn):
    m, kd = xg.shape
    kp = p.shape[1]
    n = w_gate.shape[1]
    return pl.pallas_call(
        functools.partial(_ple_body, width=kd),
        grid=(m // bm, n // bn),
        in_specs=[
            pl.BlockSpec((bm, kd), lambda i, j: (i, 0)),
            pl.BlockSpec((kd, bn), lambda i, j: (0, j)),
            pl.BlockSpec((bm, kp), lambda i, j: (i, 0)),
            pl.BlockSpec((kp, bn), lambda i, j: (0, j)),
            pl.BlockSpec((bm, bn), lambda i, j: (i, j)),
            pl.BlockSpec((bm, ss.shape[1]), lambda i, j: (i, 0)),
        ],
        out_specs=pl.BlockSpec((bm, bn), lambda i, j: (i, j)),
        out_shape=jax.ShapeDtypeStruct((m, n), F32),
        compiler_params=_cparams(("parallel", "parallel")),
        name="ple",
    )(xg, w_gate, p, w_proj, x, ss)


def _norm_rope(x, gain, cos, sin_signed):
    y = x * lax.rsqrt(jnp.mean(x * x, axis=-1, keepdims=True) + NORM_EPS) * gain
    return y * cos + pltpu.roll(y, HEAD_DIM // 2, 1) * sin_signed


def _nsa_prep_body(q_ref, ks_ref, vs_ref, kw_ref, vw_ref, cos_ref, sin_ref, qg_ref, ksg_ref, kwg_ref,
                   qo_ref, ksat_ref, vso_ref, kwt_ref, vwo_ref, *, rows, blocks_per_seq):
    cos = cos_ref[...]
    sin = sin_ref[...]
    q_scale = (HEAD_DIM ** -0.5) * LOG2_E
    for h in range(NSA_HEADS):
        sl = slice(h * HEAD_DIM, (h + 1) * HEAD_DIM)
        qo_ref[:, sl] = (_norm_rope(q_ref[:, sl], qg_ref[...], cos, sin) * q_scale).astype(BF16)
    t = (pl.program_id(0) % blocks_per_seq) * rows + _iota((LANES, rows), 1)
    onehot_t = (_iota((LANES, rows), 0) == t // SEL_BLOCK).astype(BF16)
    for g in range(NSA_KV_HEADS):
        sl = slice(g * HEAD_DIM, (g + 1) * HEAD_DIM)
        ksat_ref[g, 0:HEAD_DIM, :] = _norm_rope(ks_ref[:, sl], ksg_ref[...], cos, sin).T.astype(BF16)
        ksat_ref[g, HEAD_DIM:2 * HEAD_DIM, :] = onehot_t
        kwt_ref[g] = _norm_rope(kw_ref[:, sl], kwg_ref[...], cos, sin).T.astype(BF16)
    vso_ref[...] = vs_ref[...].astype(BF16)
    vwo_ref[...] = vw_ref[...].astype(BF16)


def _nsa_prep(proj, cos, sin_signed, q_gain, ks_gain, kw_gain, *, batch, seq, rows=256):
    m = proj.shape[0]
    bps = seq // rows
    kvw = NSA_KV_WIDTH
    kv_blk = OFF_NKV // kvw

    def col(c):
        return lambda i: (i, c)

    tab = pl.BlockSpec((rows, HEAD_DIM), lambda i: (i % bps, 0))
    gain = pl.BlockSpec((1, HEAD_DIM), lambda i: (0, 0))
    return pl.pallas_call(
        functools.partial(_nsa_prep_body, rows=rows, blocks_per_seq=bps),
        grid=(m // rows,),
        in_specs=[
            pl.BlockSpec((rows, NSA_WIDTH), col(OFF_NQ // NSA_WIDTH)),
            pl.BlockSpec((rows, kvw), col(kv_blk + 2)),
            pl.BlockSpec((rows, kvw), col(kv_blk + 3)),
            pl.BlockSpec((rows, kvw), col(kv_blk + 4)),
            pl.BlockSpec((rows, kvw), col(kv_blk + 5)),
            tab, tab, gain, gain, gain,
        ],
        out_specs=[
            pl.BlockSpec((rows, NSA_WIDTH), col(0)),
            pl.BlockSpec((None, NSA_KV_HEADS, 2 * HEAD_DIM, rows), lambda i: (i // bps, 0, 0, i % bps)),
            pl.BlockSpec((rows, kvw), col(0)),
            pl.BlockSpec((None, NSA_KV_HEADS, HEAD_DIM, rows), lambda i: (i // bps, 0, 0, i % bps)),
            pl.BlockSpec((rows, kvw), col(0)),
        ],
        out_shape=[
            jax.ShapeDtypeStruct((m, NSA_WIDTH), BF16),
            jax.ShapeDtypeStruct((batch, NSA_KV_HEADS, 2 * HEAD_DIM, seq), BF16),
            jax.ShapeDtypeStruct((m, kvw), BF16),
            jax.ShapeDtypeStruct((batch, NSA_KV_HEADS, HEAD_DIM, seq), BF16),
            jax.ShapeDtypeStruct((m, kvw), BF16),
        ],
        compiler_params=_cparams(("parallel",)),
        name="nsa_prep",
    )(proj, proj, proj, proj, proj, cos, sin_signed, q_gain.reshape(1, -1), ks_gain.reshape(1, -1), kw_gain.reshape(1, -1))


def _gelu_tanh(x):
    return 0.5 * x * (1.0 + jnp.tanh(np.sqrt(2.0 / np.pi).astype(np.float32) * (x + 0.044715 * (x * x * x))))


def _compress_one(x_ref, pe_ref, w1_ref, w2_ref, n_chunks):
    half = CMP_LEN // 2
    first = jnp.zeros((n_chunks, CMP_HIDDEN), F32)
    second = jnp.zeros((n_chunks, CMP_HIDDEN), F32)
    for l in range(half):
        xl = x_ref[pl.ds(l, n_chunks, stride=CMP_STRIDE), :]
        first += _dot((xl + pe_ref[l:l + 1, :]).astype(BF16), w1_ref[l])
        second += _dot((xl + pe_ref[half + l:half + l + 1, :]).astype(BF16), w1_ref[half + l])
    hid = _gelu_tanh(first + pltpu.roll(second, n_chunks - 1, 0))
    return _dot(hid.astype(BF16), w2_ref[...])


def _compress_body(xk_ref, xv_ref, pek_ref, pev_ref, wk1_ref, wk2_ref, wv1_ref, wv2_ref, kg_ref, cos_ref, sin_ref,
                   kto_ref, vo_ref, *, n_chunks):
    k = _compress_one(xk_ref, pek_ref, wk1_ref, wk2_ref, n_chunks)
    kto_ref[...] = _norm_rope(k, kg_ref[...], cos_ref[...], sin_ref[...]).T.astype(BF16)
    vo_ref[...] = _compress_one(xv_ref, pev_ref, wv1_ref, wv2_ref, n_chunks).astype(BF16)


def _compress(proj, pe_k, pe_v, wk1, wk2, wv1, wv2, kc_gain, cos_c, sin_c, *, batch, seq):
    n_chunks = seq // CMP_STRIDE
    kc_blk = OFF_NKV // HEAD_DIM
    vc_blk = (OFF_NKV + NSA_KV_WIDTH) // HEAD_DIM

    def full(shape):
        return pl.BlockSpec(shape, lambda b, g: (0,) * len(shape))

    return pl.pallas_call(
        functools.partial(_compress_body, n_chunks=n_chunks),
        grid=(batch, NSA_KV_HEADS),
        in_specs=[
            pl.BlockSpec((seq, HEAD_DIM), lambda b, g: (b, kc_blk + g)),
            pl.BlockSpec((seq, HEAD_DIM), lambda b, g: (b, vc_blk + g)),
            full((CMP_LEN, HEAD_DIM)), full((CMP_LEN, HEAD_DIM)),
            full((CMP_LEN, HEAD_DIM, CMP_HIDDEN)), full((CMP_HIDDEN, HEAD_DIM)),
            full((CMP_LEN, HEAD_DIM, CMP_HIDDEN)), full((CMP_HIDDEN, HEAD_DIM)),
            full((1, HEAD_DIM)), full((n_chunks, HEAD_DIM)), full((n_chunks, HEAD_DIM)),
        ],
        out_specs=[
            pl.BlockSpec((None, None, HEAD_DIM, n_chunks), lambda b, g: (b, g, 0, 0)),
            pl.BlockSpec((None, None, n_chunks, HEAD_DIM), lambda b, g: (b, g, 0, 0)),
        ],
        out_shape=[
            jax.ShapeDtypeStruct((batch, NSA_KV_HEADS, HEAD_DIM, n_chunks), BF16),
            jax.ShapeDtypeStruct((batch, NSA_KV_HEADS, n_chunks, HEAD_DIM), BF16),
        ],
        compiler_params=_cparams(("parallel", "parallel")),
        name="nsa_compress",
    )(proj, proj, pe_k, pe_v, wk1, wk2, wv1, wv2, kc_gain.reshape(1, -1), cos_c, sin_c)


def _softmax2(s, mask):
    s = jnp.where(mask, s, -MASK_BIG)
    e = jnp.where(mask, jnp.exp2(s - jnp.max(s, axis=-1, keepdims=True)), 0.0)
    return e * (1.0 / jnp.maximum(jnp.sum(e, axis=-1, keepdims=True), 1e-30))


def _select_blocks(imp, t0, top_k):
    tq = imp.shape[0]
    imp_t = imp.T
    n_lane = imp_t.shape[0]
    j = _iota((n_lane, tq), 0)
    cur = (t0 + _iota((n_lane, tq), 1)) // SEL_BLOCK
    valid = j <= cur
    forced = (j == 0) | (j == cur) | (j == cur - 1)
    score = jnp.where(valid & jnp.logical_not(forced), imp_t, -jnp.inf)
    for _ in range(top_k - 3):
        best = jnp.max(score, axis=0, keepdims=True)
        first = jnp.min(jnp.where(score == best, j, n_lane), axis=0, keepdims=True)
        score = jnp.where(j == first, -jnp.inf, score)
    sel = jnp.where(valid & (score == -jnp.inf), 0.0, -BIAS_BIG)
    return sel.T.astype(BF16)


def _lane_column(x, lane):
    return jnp.sum(jnp.where(_iota(x.shape, 1) == lane, x, 0.0), axis=1, keepdims=True)


def _flash_scores(s, m_ref, l_ref, r):
    chunks = [s[:, c * LANES:(c + 1) * LANES] for c in range(s.shape[1] // LANES)]
    m_prev = m_ref[r]
    row_max = jnp.max(functools.reduce(jnp.maximum, chunks), axis=-1, keepdims=True)
    m_new = jnp.maximum(m_prev, jnp.broadcast_to(row_max, m_prev.shape))
    alpha = jnp.exp2(m_prev - m_new)
    ps = [jnp.exp2(ch - m_new) for ch in chunks]
    l_ref[r] = alpha * l_ref[r] + functools.reduce(jnp.add, ps)
    m_ref[r] = m_new
    return jnp.concatenate([x.astype(BF16) for x in ps], axis=1), alpha


def _attend_body(q_ref, kct_ref, vc_ref, ov_ref, ksat_ref, vs_ref, kwt_ref, vw_ref, sm_ref, o_ref,
                 m_scr, l_scr, acc_scr, oc_scr, *, tq, tk, tail, top_k):
    g = pl.program_id(1)
    t0 = pl.program_id(2) * tq
    tpos = t0 + _iota((tq, 1), 0)
    heads = range(NSA_GROUP)
    lanes = [slice(r * HEAD_DIM, (r + 1) * HEAD_DIM) for r in heads]
    q = [q_ref[:, sl] for sl in lanes]

    n_cmp = kct_ref.shape[1]
    visible = _iota((1, n_cmp), 1) * CMP_STRIDE + (CMP_LEN - 1) <= tpos
    kct = kct_ref[...]
    vc = vc_ref[...]
    probs = [_softmax2(s, visible) for s in [_dot(x, kct) for x in q]]
    for r, p in zip(heads, probs):
        oc_scr[r] = _dot(p.astype(BF16), vc)
    ov = ov_ref[...]
    p_hi, p_mid, p_lo = _split3(functools.reduce(jnp.add, probs))
    imp = _dot(p_hi, ov) + (_dot(p_mid, ov) + _dot(p_lo, ov))
    sb = _select_blocks(imp, t0, top_k)

    qa = [jnp.concatenate([x, sb], axis=1) for x in q]
    m_scr[...] = jnp.full(m_scr.shape, -MASK_BIG, F32)
    l_scr[...] = jnp.zeros(l_scr.shape, F32)
    acc_scr[...] = jnp.zeros(acc_scr.shape, F32)

    def tile(k0, width, causal):
        k_tile = ksat_ref[:, pl.ds(k0, width)]
        v_tile = vs_ref[pl.ds(k0, width), :]
        scores = [_dot(x, k_tile) for x in qa]
        if causal:
            keep = k0 + _iota((1, width), 1) <= tpos
            scores = [jnp.where(keep, s, -MASK_BIG) for s in scores]
        stats = [_flash_scores(scores[r], m_scr, l_scr, r) for r in heads]
        for r, (p, alpha) in zip(heads, stats):
            acc_scr[r] = alpha * acc_scr[r] + _dot(p, v_tile)

    n_full = t0 // tk

    def full_body(kt, carry):
        tile(pl.multiple_of(kt * tk, tk), tk, False)
        return carry

    lax.fori_loop(0, n_full, full_body, 0)
    done = n_full * tk

    def tail_body(j, carry):
        tile(pl.multiple_of(done + j * tail, tail), tail, True)
        return carry

    lax.fori_loop(0, (t0 + tq - done + tail - 1) // tail, tail_body, 0)

    band = WINDOW + tq
    s0 = pl.multiple_of(jnp.maximum(t0 - WINDOW, 0), tq)
    kw_band = kwt_ref[:, pl.ds(s0, band)]
    vw_band = vw_ref[pl.ds(s0, band), :]
    dist = tpos - (s0 + _iota((1, band), 1))
    in_window = (dist >= 0) & (dist < WINDOW)
    probs = [_softmax2(s, in_window).astype(BF16) for s in [_dot(x, kw_band) for x in q]]
    o_w = [_dot(p, vw_band) for p in probs]

    gates = _sigmoid(sm_ref[...])
    for r, sl in zip(heads, lanes):
        o_s = acc_scr[r] * (1.0 / jnp.maximum(jnp.sum(l_scr[r], axis=-1, keepdims=True), 1e-30))
        lane = SM_GATE + (g * NSA_GROUP + r) * 3
        out = (_lane_column(gates, lane) * oc_scr[r] + _lane_column(gates, lane + 1) * o_s
               + _lane_column(gates, lane + 2) * o_w[r])
        o_ref[:, sl] = out.astype(BF16)


def _attend(qn, k_cmp_t, v_cmp, overlap, ksa_t, vs, kw_t, vw, small, *, batch, seq, tq=256, tk=1024, tail=512):
    m = qn.shape[0]
    gw = NSA_GROUP * HEAD_DIM
    nq = seq // tq
    tk = min(tk, seq)
    n_cmp = v_cmp.shape[2]
    top_k = min(SEL_TOPK, seq // SEL_BLOCK)

    def tile_spec(width):
        return pl.BlockSpec((tq, width), lambda b, g, i: (b * nq + i, g))

    def seq_spec(width):
        return pl.BlockSpec((seq, width), lambda b, g, i: (b, g))

    def group_spec(rows, cols):
        return pl.BlockSpec((None, None, rows, cols), lambda b, g, i: (b, g, 0, 0))

    stat = pltpu.VMEM((NSA_GROUP, tq, LANES), F32)
    return pl.pallas_call(
        functools.partial(_attend_body, tq=tq, tk=tk, tail=tail, top_k=top_k),
        grid=(batch, NSA_KV_HEADS, nq),
        in_specs=[
            tile_spec(gw),
            group_spec(HEAD_DIM, n_cmp), group_spec(n_cmp, HEAD_DIM),
            pl.BlockSpec((n_cmp, LANES), lambda b, g, i: (0, 0)),
            group_spec(2 * HEAD_DIM, seq), seq_spec(HEAD_DIM), group_spec(HEAD_DIM, seq), seq_spec(HEAD_DIM),
            pl.BlockSpec((tq, LANES), lambda b, g, i: (b * nq + i, 0)),
        ],
        out_specs=tile_spec(gw),
        out_shape=jax.ShapeDtypeStruct((m, NSA_WIDTH), BF16),
        scratch_shapes=[stat, stat, stat, stat],
        compiler_params=_cparams(("parallel", "parallel", "arbitrary")),
        name="nsa_attend",
    )(qn, k_cmp_t, v_cmp, overlap, ksa_t, vs, kw_t, vw, small)


def _rope_tables(pos):
    half = HEAD_DIM // 2
    inv_freq = ROPE_THETA ** (-jnp.arange(half, dtype=F32) / half)
    ang = pos[:, None] * inv_freq[None, :]
    cos, sin = jnp.cos(ang), jnp.sin(ang)
    return jnp.concatenate([cos, cos], axis=-1), jnp.concatenate([-sin, sin], axis=-1)


def _overlap_matrix(n_cmp_pad, n_cmp):
    c_start = np.arange(n_cmp_pad) * CMP_STRIDE
    s_start = np.arange(LANES) * SEL_BLOCK
    ov = (c_start[:, None] < s_start[None, :] + SEL_BLOCK) & (c_start[:, None] + CMP_LEN > s_start[None, :])
    ov &= (np.arange(n_cmp_pad) < n_cmp)[:, None]
    return jnp.asarray(ov.astype(np.float32), dtype=BF16)


def _nsa(proj, small, q_gain, kc_gain, ks_gain, kw_gain, pe_k, pe_v, wk1, wk2, wv1, wv2, *, batch, seq):
    n_chunks = seq // CMP_STRIDE
    cos, sin = _rope_tables(jnp.arange(seq, dtype=F32))
    cmp_end = jnp.arange(n_chunks) * CMP_STRIDE + (CMP_LEN - 1)
    cos_c, sin_c = _rope_tables(cmp_end.astype(F32))
    qn, ksa_t, vs, kw_t, vw = _nsa_prep(proj, cos, sin, q_gain, ks_gain, kw_gain, batch=batch, seq=seq)
    k_cmp_t, v_cmp = _compress(proj, pe_k, pe_v, wk1.astype(BF16), wk2.astype(BF16), wv1.astype(BF16),
                               wv2.astype(BF16), kc_gain, cos_c, sin_c, batch=batch, seq=seq)
    overlap = _overlap_matrix(n_chunks, n_chunks - 1)
    return _attend(qn, k_cmp_t, v_cmp, overlap, ksa_t, vs, kw_t, vw, small, batch=batch, seq=seq)


def _unit_lower_inverses(mats):
    c = mats[0].shape[0]
    row = _iota((c, c), 0)
    col = _iota((c, c), 1)
    eye = (row == col).astype(F32)
    diag = row // INV_BASE == col // INV_BASE
    xs = [jnp.where(diag, -a, 0.0) for a in mats]
    ts = [eye + x for x in xs]
    x16 = [x.astype(BF16) for x in xs]
    powers = [_dot(x, x) for x in x16]
    span = 2
    while span < INV_BASE:
        p16 = [p.astype(BF16) for p in powers]
        if 2 * span < INV_BASE:
            both = [_dot(p, jnp.concatenate([p, t.astype(BF16)], axis=1)) for p, t in zip(p16, ts)]
            ts = [t + b[:, c:] for t, b in zip(ts, both)]
            powers = [b[:, :c] for b in both]
        else:
            ts = [t + _dot(p, t.astype(BF16)) for p, t in zip(p16, ts)]
        span *= 2
    size = INV_BASE
    while size < c:
        off = (row // (2 * size) == col // (2 * size)) & (row // size != col // size)
        t16 = [t.astype(BF16) for t in ts]
        inner = [_dot(jnp.where(off, a, 0.0).astype(BF16), t) for a, t in zip(mats, t16)]
        ts = [t - _dot(t2, i.astype(BF16)) for t, t2, i in zip(ts, t16, inner)]
        size *= 2
    resid = [eye - _dot_hp(_split2(eye + a), _split2(t)) for a, t in zip(mats, ts)]
    return [t + _dot(t.astype(BF16), r.astype(BF16)) for t, r in zip(ts, resid)]


def _gdn_body(xq_ref, xk_ref, xv_ref, z_ref, sm_ref, wq_ref, wk_ref, wv_ref, alog_ref, dtb_ref, og_ref, o_ref,
              xs_scr, state_scr, *, heads):
    c = GDN_CHUNK
    hb = pl.program_id(1)
    step = pl.program_id(2)
    halo = 8
    hs = range(heads)

    @pl.when(step == 0)
    def _():
        xs_scr[:, 0:halo, :] = jnp.zeros((3, halo, xs_scr.shape[2]), F32)
        state_scr[...] = jnp.zeros_like(state_scr)

    conv = []
    for idx, (x_ref, w_ref) in enumerate(((xq_ref, wq_ref), (xk_ref, wk_ref), (xv_ref, wv_ref))):
        xs_scr[idx, halo:halo + c, :] = x_ref[...]
        y = jnp.zeros(x_ref.shape, F32)
        for tap in range(GDN_CONV):
            y = y + w_ref[tap:tap + 1, :] * xs_scr[idx, pl.ds(halo - (GDN_CONV - 1) + tap, c), :]
        xs_scr[idx, 0:halo, :] = xs_scr[idx, c:c + halo, :]
        conv.append(_silu(y))
    qc, kc, vc = conv

    sm = sm_ref[...]
    beta_all = _sigmoid(sm)
    x = sm + dtb_ref[...]
    softplus = jnp.maximum(x, 0.0) + jnp.log1p(jnp.exp(-jnp.abs(x)))
    g_all = -(jnp.exp(alog_ref[...]) * softplus)
    row = _iota((c, c), 0)
    col = _iota((c, c), 1)
    tril = row >= col
    ones_tril = tril.astype(BF16)
    g_hi, g_mid, g_lo = _split3(g_all)
    gc_all = _dot(ones_tril, g_hi) + (_dot(ones_tril, g_mid) + _dot(ones_tril, g_lo))
    gc_all_t = gc_all.T

    def head_lanes(h):
        return slice(h * HEAD_DIM, (h + 1) * HEAD_DIM)

    qs, ks, gammas, g_cols, g_lasts, betas = [], [], [], [], [], []
    for h in hs:
        q = qc[:, head_lanes(h)]
        k = kc[:, head_lanes(h)]
        qs.append(q * lax.rsqrt(jnp.sum(q * q, axis=-1, keepdims=True) + NORM_EPS) * (HEAD_DIM ** -0.5))
        ks.append(k * lax.rsqrt(jnp.sum(k * k, axis=-1, keepdims=True) + NORM_EPS))
        head = hb * heads + h
        betas.append(_lane_column(beta_all, SM_BETA + head))
        g_col = _lane_column(gc_all, SM_DECAY + head)
        g_row = jnp.sum(jnp.where(_iota((LANES, c), 0) == SM_DECAY + head, gc_all_t, 0.0), axis=0, keepdims=True)
        gammas.append(jnp.exp(jnp.where(tril, g_col - g_row, -MASK_BIG)))
        g_cols.append(g_col)
        g_lasts.append(g_row[:, c - 1:c])
    decays = [jnp.exp(g) for g in g_cols]
    kbs = [k * b for k, b in zip(ks, betas)]
    k16 = [k.astype(BF16) for k in ks]
    kk_qk = [_dot_nt(jnp.concatenate([kb, q], axis=0).astype(BF16), kk) for kb, q, kk in zip(kbs, qs, k16)]
    t_inv = _unit_lower_inverses([jnp.where(row > col, x[:c] * gm, 0.0) for x, gm in zip(kk_qk, gammas)])
    uw = [_dot(t.astype(BF16), jnp.concatenate([vc[:, head_lanes(h)] * b, kb * d], axis=1).astype(BF16))
          for h, t, b, kb, d in zip(hs, t_inv, betas, kbs, decays)]
    states = [state_scr[h] for h in hs]
    ws_qs = [_dot(jnp.concatenate([u[:, HEAD_DIM:], q * d], axis=0).astype(BF16), s.astype(BF16))
             for u, q, d, s in zip(uw, qs, decays, states)]
    v_new = [(u[:, :HEAD_DIM] - w[:c]).astype(BF16) for u, w in zip(uw, ws_qs)]
    outs = [w[c:] + _dot((x[c:] * gm).astype(BF16), v) for w, x, gm, v in zip(ws_qs, kk_qk, gammas, v_new)]
    for h in hs:
        k_tail = (ks[h] * jnp.exp(g_lasts[h] - g_cols[h])).astype(BF16)
        state_scr[h] = states[h] * jnp.exp(g_lasts[h]) + _dot_tn(k_tail, v_new[h])
    for h in hs:
        o = outs[h]
        o = o * lax.rsqrt(jnp.mean(o * o, axis=-1, keepdims=True) + NORM_EPS) * og_ref[...]
        o_ref[:, head_lanes(h)] = (o * _silu(z_ref[:, head_lanes(h)])).astype(BF16)


def _lane_row(values, offset):
    return jnp.zeros((1, LANES), F32).at[0, offset:offset + values.shape[0]].set(values)


def _gdn(proj, small, conv_w, a_log, dt_bias, o_gain, *, batch, seq):
    m = proj.shape[0]
    c = GDN_CHUNK
    heads = GDN_HEAD_BLOCK
    width = heads * HEAD_DIM
    n_steps = seq // c

    def act_spec(off):
        base = off // width
        return pl.BlockSpec((c, width), lambda b, hb, s: (b * n_steps + s, base + hb))

    def conv_spec(part):
        base = part * GDN_WIDTH // width
        return pl.BlockSpec((GDN_CONV, width), lambda b, hb, s: (0, base + hb))

    row_spec = pl.BlockSpec((1, LANES), lambda b, hb, s: (0, 0))
    return pl.pallas_call(
        functools.partial(_gdn_body, heads=heads),
        grid=(batch, GDN_HEADS // heads, n_steps),
        in_specs=[
            act_spec(OFF_GQKV), act_spec(OFF_GQKV + GDN_WIDTH), act_spec(OFF_GQKV + 2 * GDN_WIDTH), act_spec(OFF_GZ),
            pl.BlockSpec((c, LANES), lambda b, hb, s: (b * n_steps + s, 0)),
            conv_spec(0), conv_spec(1), conv_spec(2),
            row_spec, row_spec, row_spec,
        ],
        out_specs=pl.BlockSpec((c, width), lambda b, hb, s: (b * n_steps + s, hb)),
        out_shape=jax.ShapeDtypeStruct((m, GDN_WIDTH), BF16),
        scratch_shapes=[pltpu.VMEM((3, c + 8, width), F32), pltpu.VMEM((heads, HEAD_DIM, HEAD_DIM), F32)],
        compiler_params=_cparams(("parallel", "parallel", "arbitrary")),
        name="gdn",
    )(proj, proj, proj, proj, small, conv_w, conv_w, conv_w,
      _lane_row(a_log, SM_DECAY), _lane_row(dt_bias, SM_DECAY), o_gain.reshape(1, -1))


def _split_w_in(w_in):
    sizes = (NSA_WIDTH, 6 * NSA_KV_WIDTH, 3 * NSA_HEADS, 3 * GDN_WIDTH, GDN_WIDTH, GDN_HEADS, GDN_HEADS, 2 * D_MODEL)
    offs = np.cumsum((0,) + sizes)
    w16 = w_in.astype(BF16)
    nq, nkv, ngate, gqkv, gz, gbeta, gdecay, merge = (w16[:, offs[i]:offs[i + 1]] for i in range(len(sizes)))
    big = jnp.concatenate([merge, gqkv, nq, gz, nkv], axis=1)
    pad = jnp.zeros((w_in.shape[0], LANES - (3 * NSA_HEADS + 2 * GDN_HEADS)), BF16)
    small = jnp.concatenate([ngate, gbeta, gdecay, pad], axis=1)
    return big, small


def _layer(x, p, g_mix, w_in, nsa_q_gain, nsa_kc_gain, nsa_ks_gain, nsa_kw_gain, cmp_pe_k, cmp_pe_v, cmp_wk1, cmp_wk2,
           cmp_wv1, cmp_wv2, gdn_conv_w, gdn_a_log, gdn_dt_bias, gdn_o_gain, w_up_nsa, w_up_gdn, w_out, g_mlp,
           w_mlp_in, w_mlp_out, g_ple, w_ple_gate, w_ple_proj, *, batch, seq):
    h = _rmsnorm(x, g_mix)
    w_big, w_small = _split_w_in(w_in)
    proj = _matmul(h, w_big, out_dtype=F32, name="in_proj", **DENSE_TILES["in_proj"])
    small = _matmul(h, w_small, out_dtype=F32, name="in_proj_small", **DENSE_TILES["in_proj_small"])
    o_a = _nsa(proj, small, nsa_q_gain, nsa_kc_gain, nsa_ks_gain, nsa_kw_gain, cmp_pe_k, cmp_pe_v, cmp_wk1, cmp_wk2,
               cmp_wv1, cmp_wv2, batch=batch, seq=seq)
    o_b = _gdn(proj, small, gdn_conv_w, gdn_a_log, gdn_dt_bias, gdn_o_gain, batch=batch, seq=seq)
    mixed = _merge(o_a, w_up_nsa.astype(BF16), o_b, w_up_gdn.astype(BF16), proj, **DENSE_TILES["merge_up"])
    x, xg, ss = _out_proj(mixed, w_out.astype(BF16), x, g_mlp, **DENSE_TILES["out_proj"])
    hidden = _mlp_in(xg, w_mlp_in.astype(BF16), ss, **DENSE_TILES["mlp_in"])
    x, xg, ss = _mlp_out(hidden, w_mlp_out.astype(BF16), x, g_ple, **DENSE_TILES["mlp_out"])
    return _ple(xg, w_ple_gate.astype(BF16), p.astype(BF16), w_ple_proj.astype(BF16), x, ss, **DENSE_TILES["ple"])


def kernel(x, p, g_mix, w_in, nsa_q_gain, nsa_kc_gain, nsa_ks_gain, nsa_kw_gain, cmp_pe_k, cmp_pe_v, cmp_wk1, cmp_wk2,
           cmp_wv1, cmp_wv2, gdn_conv_w, gdn_a_log, gdn_dt_bias, gdn_o_gain, w_up_nsa, w_up_gdn, w_out, g_mlp,
           w_mlp_in, w_mlp_out, g_ple, w_ple_gate, w_ple_proj):
    batch, seq, d = x.shape
    depth = p.shape[0]
    assert seq // SEL_BLOCK <= LANES and seq % (4 * LANES) == 0
    y = x.reshape(batch * seq, d)
    for i in range(depth):
        y = _layer(y, p[i].reshape(batch * seq, -1), g_mix[i], w_in[i], nsa_q_gain[i], nsa_kc_gain[i], nsa_ks_gain[i],
                   nsa_kw_gain[i], cmp_pe_k[i], cmp_pe_v[i], cmp_wk1[i], cmp_wk2[i], cmp_wv1[i], cmp_wv2[i],
                   gdn_conv_w[i], gdn_a_log[i], gdn_dt_bias[i], gdn_o_gain[i], w_up_nsa[i], w_up_gdn[i], w_out[i],
                   g_mlp[i], w_mlp_in[i], w_mlp_out[i], g_ple[i], w_ple_gate[i], w_ple_proj[i], batch=batch, seq=seq)
    return y.reshape(batch, seq, d)
```

```python
import functools

import numpy as np
import jax
import jax.numpy as jnp
from jax import lax
from jax.experimental import pallas as pl
from jax.experimental.pallas import tpu as pltpu

D_MODEL = 4096
HEAD_DIM = 128
NSA_HEADS = 16
NSA_KV_HEADS = 4
NSA_GROUP = NSA_HEADS // NSA_KV_HEADS
CMP_LEN = 32
CMP_STRIDE = 16
CMP_HIDDEN = 4 * HEAD_DIM
SEL_BLOCK = 64
SEL_TOPK = 16
WINDOW = 512
GDN_HEADS = 16
GDN_CONV = 4
MLP_HIDDEN = 4 * D_MODEL
PLE_DIM = 256
ROPE_THETA = 10000.0
NORM_EPS = 1e-6
MASK_BIG = 1e30
BIAS_BIG = 2.0 ** 99
LOG2_E = 1.4426950408889634

NSA_WIDTH = NSA_HEADS * HEAD_DIM
NSA_KV_WIDTH = NSA_KV_HEADS * HEAD_DIM
GDN_WIDTH = GDN_HEADS * HEAD_DIM
LANES = 128
GDN_CHUNK = 128
GDN_HEAD_BLOCK = 16
INV_BASE = 16

OFF_MERGE = 0
OFF_GQKV = OFF_MERGE + 2 * D_MODEL
OFF_NQ = OFF_GQKV + 3 * GDN_WIDTH
OFF_GZ = OFF_NQ + NSA_WIDTH
OFF_NKV = OFF_GZ + GDN_WIDTH
BIG_WIDTH = OFF_NKV + 6 * NSA_KV_WIDTH
SM_GATE = 0
SM_BETA = 3 * NSA_HEADS
SM_DECAY = SM_BETA + GDN_HEADS

V7X_VMEM_LIMIT = 56 * 1024 * 1024

DENSE_TILES = {
    "in_proj": dict(bm=1024, bn=1024),
    "in_proj_small": dict(bm=1024, bn=LANES),
    "merge_up": dict(bm=1024, bn=512),
    "out_proj": dict(bm=1024, bn=512),
    "mlp_in": dict(bm=1024, bn=1024),
    "mlp_out": dict(bm=1024, bn=1024, bk=2048),
    "ple": dict(bm=1024, bn=512),
}

F32 = jnp.float32
BF16 = jnp.bfloat16


def _cparams(sem, vmem=V7X_VMEM_LIMIT):
    return pltpu.CompilerParams(dimension_semantics=sem, vmem_limit_bytes=vmem)


def _dot(a, b):
    return jnp.dot(a, b, preferred_element_type=F32)


def _dot_nt(a, b):
    return lax.dot_general(a, b, (((1,), (1,)), ((), ())), preferred_element_type=F32)


def _dot_tn(a, b):
    return lax.dot_general(a, b, (((0,), (0,)), ((), ())), preferred_element_type=F32)


def _split3(x):
    hi = x.astype(BF16)
    r = x - hi.astype(F32)
    mid = r.astype(BF16)
    lo = (r - mid.astype(F32)).astype(BF16)
    return hi, mid, lo


def _split2(x):
    hi = x.astype(BF16)
    lo = (x - hi.astype(F32)).astype(BF16)
    return hi, lo


def _dot_hp(a, b):
    return _dot(a[0], b[0]) + (_dot(a[0], b[1]) + _dot(a[1], b[0]))


def _sigmoid(x):
    return 0.5 * jnp.tanh(0.5 * x) + 0.5


def _silu(x):
    h = 0.5 * x
    return h * jnp.tanh(h) + h


def _iota(shape, dim):
    return lax.broadcasted_iota(jnp.int32, shape, dim)


def _rmsnorm_body(x_ref, g_ref, o_ref):
    x = x_ref[...]
    y = x * lax.rsqrt(jnp.mean(x * x, axis=-1, keepdims=True) + NORM_EPS)
    o_ref[...] = (y * g_ref[...]).astype(o_ref.dtype)


def _rmsnorm(x, gain, *, rows=256):
    m, d = x.shape
    return pl.pallas_call(
        _rmsnorm_body,
        grid=(m // rows,),
        in_specs=[pl.BlockSpec((rows, d), lambda i: (i, 0)), pl.BlockSpec((1, d), lambda i: (0, 0))],
        out_specs=pl.BlockSpec((rows, d), lambda i: (i, 0)),
        out_shape=jax.ShapeDtypeStruct((m, d), BF16),
        compiler_params=_cparams(("parallel",)),
        name="rmsnorm",
    )(x, gain.reshape(1, d))


def _row_spec(bm, width):
    return pl.BlockSpec((bm, width), lambda i, j: (i, 0))


def _mm_nt_body(a_ref, bt_ref, o_ref):
    o_ref[...] = _dot_nt(a_ref[...], bt_ref[...]).astype(o_ref.dtype)


def _matmul_nt(a, b_t, *, bm, bn, out_dtype, name):
    m, kdim = a.shape
    n = b_t.shape[0]
    return pl.pallas_call(
        _mm_nt_body,
        grid=(m // bm, n // bn),
        in_specs=[_row_spec(bm, kdim), pl.BlockSpec((bn, kdim), lambda i, j: (j, 0))],
        out_specs=pl.BlockSpec((bm, bn), lambda i, j: (i, j)),
        out_shape=jax.ShapeDtypeStruct((m, n), out_dtype),
        compiler_params=_cparams(("parallel", "parallel")),
        name=name,
    )(a, b_t)


def _norm_outputs(x, g_ref, xg_ref, ss_ref):
    xg_ref[...] = (x * g_ref[...]).astype(BF16)
    sq = x * x
    ss_ref[...] = functools.reduce(jnp.add, [sq[:, c * LANES:(c + 1) * LANES] for c in range(sq.shape[1] // LANES)])


def _row_scale(ss_ref, width):
    return lax.rsqrt(jnp.sum(ss_ref[...], axis=-1, keepdims=True) * (1.0 / width) + NORM_EPS)


def _norm_out_shapes(m, n, bn):
    return [jax.ShapeDtypeStruct((m, n), F32), jax.ShapeDtypeStruct((m, n), BF16),
            jax.ShapeDtypeStruct((m, n // bn * LANES), F32)]


def _out_proj_body(a_ref, b_ref, x_ref, g_ref, o_ref, xg_ref, ss_ref):
    x = x_ref[...] + _dot(a_ref[...], b_ref[...])
    o_ref[...] = x
    _norm_outputs(x, g_ref, xg_ref, ss_ref)


def _out_proj(a, b, x, gain, *, bm, bn):
    m, kdim = a.shape
    n = b.shape[1]
    tile = pl.BlockSpec((bm, bn), lambda i, j: (i, j))
    return pl.pallas_call(
        _out_proj_body,
        grid=(m // bm, n // bn),
        in_specs=[_row_spec(bm, kdim), pl.BlockSpec((kdim, bn), lambda i, j: (0, j)),
                  tile, pl.BlockSpec((1, bn), lambda i, j: (0, j))],
        out_specs=[tile, tile, pl.BlockSpec((bm, LANES), lambda i, j: (i, j))],
        out_shape=_norm_out_shapes(m, n, bn),
        compiler_params=_cparams(("parallel", "parallel")),
        name="out_proj",
    )(a, b, x, gain.reshape(1, n))


def _mlp_in_body(a_ref, b_ref, ss_ref, o_ref, *, width):
    r = jnp.maximum(_dot(a_ref[...], b_ref[...]) * _row_scale(ss_ref, width), 0.0)
    o_ref[...] = (r * r).astype(o_ref.dtype)


def _mlp_in(xg, b, ss, *, bm, bn):
    m, kdim = xg.shape
    n = b.shape[1]
    return pl.pallas_call(
        functools.partial(_mlp_in_body, width=kdim),
        grid=(m // bm, n // bn),
        in_specs=[pl.BlockSpec((bm, kdim), lambda i, j: (i, 0)), pl.BlockSpec((kdim, bn), lambda i, j: (0, j)),
                  pl.BlockSpec((bm, ss.shape[1]), lambda i, j: (i, 0))],
        out_specs=pl.BlockSpec((bm, bn), lambda i, j: (i, j)),
        out_shape=jax.ShapeDtypeStruct((m, n), BF16),
        compiler_params=_cparams(("parallel", "parallel")),
        name="mlp_in",
    )(xg, b, ss)


def _mlp_out_body(a_ref, b_ref, res_ref, g_ref, o_ref, xg_ref, ss_ref, *, nk):
    k = pl.program_id(2)

    @pl.when(k == 0)
    def _():
        o_ref[...] = res_ref[...]

    o_ref[...] += _dot(a_ref[...], b_ref[...])

    @pl.when(k == nk - 1)
    def _():
        _norm_outputs(o_ref[...], g_ref, xg_ref, ss_ref)


def _mlp_out(a, b, res, gain, *, bm, bn, bk):
    m, kdim = a.shape
    n = b.shape[1]
    nk = kdim // bk
    tile = pl.BlockSpec((bm, bn), lambda i, j, k: (i, j))
    return pl.pallas_call(
        functools.partial(_mlp_out_body, nk=nk),
        grid=(m // bm, n // bn, nk),
        in_specs=[pl.BlockSpec((bm, bk), lambda i, j, k: (i, k)), pl.BlockSpec((bk, bn), lambda i, j, k: (k, j)),
                  tile, pl.BlockSpec((1, bn), lambda i, j, k: (0, j))],
        out_specs=[tile, tile, pl.BlockSpec((bm, LANES), lambda i, j, k: (i, j))],
        out_shape=_norm_out_shapes(m, n, bn),
        compiler_params=_cparams(("parallel", "parallel", "arbitrary")),
        name="mlp_out",
    )(a, b, res, gain.reshape(1, n))


def _merge_body(oa_ref, wa_ref, ob_ref, wb_ref, ga_ref, gb_ref, o_ref):
    ya = _dot(oa_ref[...], wa_ref[...])
    yb = _dot(ob_ref[...], wb_ref[...])
    o_ref[...] = (_sigmoid(ga_ref[...]) * ya + _sigmoid(gb_ref[...]) * yb).astype(o_ref.dtype)


def _merge(o_a, w_a, o_b, w_b, proj, *, bm, bn):
    m, ka = o_a.shape
    kb = o_b.shape[1]
    n = w_a.shape[1]
    ga_blk = OFF_MERGE // bn
    gb_blk = (OFF_MERGE + D_MODEL) // bn
    return pl.pallas_call(
        _merge_body,
        grid=(m // bm, n // bn),
        in_specs=[
            _row_spec(bm, ka),
            pl.BlockSpec((ka, bn), lambda i, j: (0, j)),
            _row_spec(bm, kb),
            pl.BlockSpec((kb, bn), lambda i, j: (0, j)),
            pl.BlockSpec((bm, bn), lambda i, j: (i, ga_blk + j)),
            pl.BlockSpec((bm, bn), lambda i, j: (i, gb_blk + j)),
        ],
        out_specs=pl.BlockSpec((bm, bn), lambda i, j: (i, j)),
        out_shape=jax.ShapeDtypeStruct((m, n), BF16),
        compiler_params=_cparams(("parallel", "parallel")),
        name="merge_up",
    )(o_a, w_a, o_b, w_b, proj, proj)


def _ple_body(xg_ref, wg_ref, p_ref, wp_ref, x_ref, ss_ref, o_ref, *, width):
    gate = _sigmoid(_dot(xg_ref[...], wg_ref[...]) * _row_scale(ss_ref, width))
    o_ref[...] = x_ref[...] + gate * _dot(p_ref[...], wp_ref[...])


def _ple(xg, w_gate, p, w_proj, x, ss, *, bm, bn):
    m, kd = xg.shape
    kp = p.shape[1]
    n = w_gate.shape[1]
    return pl.pallas_call(
        functools.partial(_ple_body, width=kd),
        grid=(m // bm, n // bn),
        in_specs=[
            _row_spec(bm, kd),
            pl.BlockSpec((kd, bn), lambda i, j: (0, j)),
            _row_spec(bm, kp),
            pl.BlockSpec((kp, bn), lambda i, j: (0, j)),
            pl.BlockSpec((bm, bn), lambda i, j: (i, j)),
            _row_spec(bm, ss.shape[1]),
        ],
        out_specs=pl.BlockSpec((bm, bn), lambda i, j: (i, j)),
        out_shape=jax.ShapeDtypeStruct((m, n), F32),
        compiler_params=_cparams(("parallel", "parallel")),
        name="ple",
    )(xg, w_gate, p, w_proj, x, ss)


def _norm_rope(x, gain, cos, sin_signed):
    y = x * lax.rsqrt(jnp.mean(x * x, axis=-1, keepdims=True) + NORM_EPS) * gain
    return y * cos + pltpu.roll(y, HEAD_DIM // 2, 1) * sin_signed


def _nsa_prep_body(q_ref, ks_ref, vs_ref, kw_ref, vw_ref, cos_ref, sin_ref, qg_ref, ksg_ref, kwg_ref,
                   qo_ref, ksat_ref, vso_ref, kwt_ref, vwo_ref, *, rows, blocks_per_seq):
    cos = cos_ref[...]
    sin = sin_ref[...]
    q_scale = (HEAD_DIM ** -0.5) * LOG2_E
    for h in range(NSA_HEADS):
        sl = slice(h * HEAD_DIM, (h + 1) * HEAD_DIM)
        qo_ref[:, sl] = (_norm_rope(q_ref[:, sl], qg_ref[...], cos, sin) * q_scale).astype(BF16)
    t = (pl.program_id(0) % blocks_per_seq) * rows + _iota((LANES, rows), 1)
    onehot_t = (_iota((LANES, rows), 0) == t // SEL_BLOCK).astype(BF16)
    for g in range(NSA_KV_HEADS):
        sl = slice(g * HEAD_DIM, (g + 1) * HEAD_DIM)
        ksat_ref[g, 0:HEAD_DIM, :] = _norm_rope(ks_ref[:, sl], ksg_ref[...], cos, sin).T.astype(BF16)
        ksat_ref[g, HEAD_DIM:2 * HEAD_DIM, :] = onehot_t
        kwt_ref[g] = _norm_rope(kw_ref[:, sl], kwg_ref[...], cos, sin).T.astype(BF16)
    vso_ref[...] = vs_ref[...].astype(BF16)
    vwo_ref[...] = vw_ref[...].astype(BF16)


def _nsa_prep(proj, cos, sin_signed, q_gain, ks_gain, kw_gain, *, batch, seq, rows=256):
    m = proj.shape[0]
    bps = seq // rows
    kvw = NSA_KV_WIDTH
    kv_blk = OFF_NKV // kvw

    def col(c):
        return lambda i: (i, c)

    tab = pl.BlockSpec((rows, HEAD_DIM), lambda i: (i % bps, 0))
    gain = pl.BlockSpec((1, HEAD_DIM), lambda i: (0, 0))
    return pl.pallas_call(
        functools.partial(_nsa_prep_body, rows=rows, blocks_per_seq=bps),
        grid=(m // rows,),
        in_specs=[
            pl.BlockSpec((rows, NSA_WIDTH), col(OFF_NQ // NSA_WIDTH)),
            pl.BlockSpec((rows, kvw), col(kv_blk + 2)),
            pl.BlockSpec((rows, kvw), col(kv_blk + 3)),
            pl.BlockSpec((rows, kvw), col(kv_blk + 4)),
            pl.BlockSpec((rows, kvw), col(kv_blk + 5)),
            tab, tab, gain, gain, gain,
        ],
        out_specs=[
            pl.BlockSpec((rows, NSA_WIDTH), col(0)),
            pl.BlockSpec((None, NSA_KV_HEADS, 2 * HEAD_DIM, rows), lambda i: (i // bps, 0, 0, i % bps)),
            pl.BlockSpec((rows, kvw), col(0)),
            pl.BlockSpec((None, NSA_KV_HEADS, HEAD_DIM, rows), lambda i: (i // bps, 0, 0, i % bps)),
            pl.BlockSpec((rows, kvw), col(0)),
        ],
        out_shape=[
            jax.ShapeDtypeStruct((m, NSA_WIDTH), BF16),
            jax.ShapeDtypeStruct((batch, NSA_KV_HEADS, 2 * HEAD_DIM, seq), BF16),
            jax.ShapeDtypeStruct((m, kvw), BF16),
            jax.ShapeDtypeStruct((batch, NSA_KV_HEADS, HEAD_DIM, seq), BF16),
            jax.ShapeDtypeStruct((m, kvw), BF16),
        ],
        compiler_params=_cparams(("parallel",)),
        name="nsa_prep",
    )(proj, proj, proj, proj, proj, cos, sin_signed, q_gain.reshape(1, -1), ks_gain.reshape(1, -1), kw_gain.reshape(1, -1))


def _gelu_tanh(x):
    return 0.5 * x * (1.0 + jnp.tanh(np.sqrt(2.0 / np.pi).astype(np.float32) * (x + 0.044715 * (x * x * x))))


def _compress_one(x_ref, pe_ref, w1_ref, w2_ref, n_chunks):
    half = CMP_LEN // 2
    first = jnp.zeros((n_chunks, CMP_HIDDEN), F32)
    second = jnp.zeros((n_chunks, CMP_HIDDEN), F32)
    for l in range(half):
        xl = x_ref[pl.ds(l, n_chunks, stride=CMP_STRIDE), :]
        first += _dot((xl + pe_ref[l:l + 1, :]).astype(BF16), w1_ref[l])
        second += _dot((xl + pe_ref[half + l:half + l + 1, :]).astype(BF16), w1_ref[half + l])
    hid = _gelu_tanh(first + pltpu.roll(second, n_chunks - 1, 0))
    return _dot(hid.astype(BF16), w2_ref[...])


def _compress_body(xk_ref, xv_ref, pek_ref, pev_ref, wk1_ref, wk2_ref, wv1_ref, wv2_ref, kg_ref, cos_ref, sin_ref,
                   kto_ref, vo_ref, *, n_chunks):
    k = _compress_one(xk_ref, pek_ref, wk1_ref, wk2_ref, n_chunks)
    kto_ref[...] = _norm_rope(k, kg_ref[...], cos_ref[...], sin_ref[...]).T.astype(BF16)
    vo_ref[...] = _compress_one(xv_ref, pev_ref, wv1_ref, wv2_ref, n_chunks).astype(BF16)


def _compress(proj, pe_k, pe_v, wk1, wk2, wv1, wv2, kc_gain, cos_c, sin_c, *, batch, seq):
    n_chunks = seq // CMP_STRIDE
    kc_blk = OFF_NKV // HEAD_DIM
    vc_blk = (OFF_NKV + NSA_KV_WIDTH) // HEAD_DIM

    def full(shape):
        return pl.BlockSpec(shape, lambda b, g: (0,) * len(shape))

    return pl.pallas_call(
        functools.partial(_compress_body, n_chunks=n_chunks),
        grid=(batch, NSA_KV_HEADS),
        in_specs=[
            pl.BlockSpec((seq, HEAD_DIM), lambda b, g: (b, kc_blk + g)),
            pl.BlockSpec((seq, HEAD_DIM), lambda b, g: (b, vc_blk + g)),
            full((CMP_LEN, HEAD_DIM)), full((CMP_LEN, HEAD_DIM)),
            full((CMP_LEN, HEAD_DIM, CMP_HIDDEN)), full((CMP_HIDDEN, HEAD_DIM)),
            full((CMP_LEN, HEAD_DIM, CMP_HIDDEN)), full((CMP_HIDDEN, HEAD_DIM)),
            full((1, HEAD_DIM)), full((n_chunks, HEAD_DIM)), full((n_chunks, HEAD_DIM)),
        ],
        out_specs=[
            pl.BlockSpec((None, None, HEAD_DIM, n_chunks), lambda b, g: (b, g, 0, 0)),
            pl.BlockSpec((None, None, n_chunks, HEAD_DIM), lambda b, g: (b, g, 0, 0)),
        ],
        out_shape=[
            jax.ShapeDtypeStruct((batch, NSA_KV_HEADS, HEAD_DIM, n_chunks), BF16),
            jax.ShapeDtypeStruct((batch, NSA_KV_HEADS, n_chunks, HEAD_DIM), BF16),
        ],
        compiler_params=_cparams(("parallel", "parallel")),
        name="nsa_compress",
    )(proj, proj, pe_k, pe_v, wk1, wk2, wv1, wv2, kc_gain.reshape(1, -1), cos_c, sin_c)


def _softmax2(s, mask):
    s = jnp.where(mask, s, -MASK_BIG)
    e = jnp.where(mask, jnp.exp2(s - jnp.max(s, axis=-1, keepdims=True)), 0.0)
    return e * (1.0 / jnp.maximum(jnp.sum(e, axis=-1, keepdims=True), 1e-30))


def _select_blocks(imp, t0, top_k):
    tq = imp.shape[0]
    imp_t = imp.T
    n_lane = imp_t.shape[0]
    j = _iota((n_lane, tq), 0)
    cur = (t0 + _iota((n_lane, tq), 1)) // SEL_BLOCK
    valid = j <= cur
    forced = (j == 0) | (j == cur) | (j == cur - 1)
    score = jnp.where(valid & jnp.logical_not(forced), imp_t, -jnp.inf)
    for _ in range(top_k - 3):
        best = jnp.max(score, axis=0, keepdims=True)
        first = jnp.min(jnp.where(score == best, j, n_lane), axis=0, keepdims=True)
        score = jnp.where(j == first, -jnp.inf, score)
    sel = jnp.where(valid & (score == -jnp.inf), 0.0, -BIAS_BIG)
    return sel.T.astype(BF16)


def _lane_column(x, lane):
    return jnp.sum(jnp.where(_iota(x.shape, 1) == lane, x, 0.0), axis=1, keepdims=True)


def _flash_scores(s, m_ref, l_ref, r):
    chunks = [s[:, c * LANES:(c + 1) * LANES] for c in range(s.shape[1] // LANES)]
    m_prev = m_ref[r]
    row_max = jnp.max(functools.reduce(jnp.maximum, chunks), axis=-1, keepdims=True)
    m_new = jnp.maximum(m_prev, jnp.broadcast_to(row_max, m_prev.shape))
    alpha = jnp.exp2(m_prev - m_new)
    ps = [jnp.exp2(ch - m_new) for ch in chunks]
    l_ref[r] = alpha * l_ref[r] + functools.reduce(jnp.add, ps)
    m_ref[r] = m_new
    return jnp.concatenate([x.astype(BF16) for x in ps], axis=1), alpha


def _attend_body(q_ref, kct_ref, vc_ref, ov_ref, ksat_ref, vs_ref, kwt_ref, vw_ref, sm_ref, o_ref,
                 m_scr, l_scr, acc_scr, oc_scr, *, tq, tk, tail, top_k):
    g = pl.program_id(1)
    t0 = pl.program_id(2) * tq
    tpos = t0 + _iota((tq, 1), 0)
    heads = range(NSA_GROUP)
    lanes = [slice(r * HEAD_DIM, (r + 1) * HEAD_DIM) for r in heads]
    q = [q_ref[:, sl] for sl in lanes]

    n_cmp = kct_ref.shape[1]
    visible = _iota((1, n_cmp), 1) * CMP_STRIDE + (CMP_LEN - 1) <= tpos
    kct = kct_ref[...]
    vc = vc_ref[...]
    probs = [_softmax2(s, visible) for s in [_dot(x, kct) for x in q]]
    for r, p in zip(heads, probs):
        oc_scr[r] = _dot(p.astype(BF16), vc)
    ov = ov_ref[...]
    p_hi, p_mid, p_lo = _split3(functools.reduce(jnp.add, probs))
    imp = _dot(p_hi, ov) + (_dot(p_mid, ov) + _dot(p_lo, ov))
    sb = _select_blocks(imp, t0, top_k)

    qa = [jnp.concatenate([x, sb], axis=1) for x in q]
    m_scr[...] = jnp.full(m_scr.shape, -MASK_BIG, F32)
    l_scr[...] = jnp.zeros(l_scr.shape, F32)
    acc_scr[...] = jnp.zeros(acc_scr.shape, F32)

    def tile(k0, width, causal):
        k_tile = ksat_ref[:, pl.ds(k0, width)]
        v_tile = vs_ref[pl.ds(k0, width), :]
        scores = [_dot(x, k_tile) for x in qa]
        if causal:
            keep = k0 + _iota((1, width), 1) <= tpos
            scores = [jnp.where(keep, s, -MASK_BIG) for s in scores]
        stats = [_flash_scores(scores[r], m_scr, l_scr, r) for r in heads]
        for r, (p, alpha) in zip(heads, stats):
            acc_scr[r] = alpha * acc_scr[r] + _dot(p, v_tile)

    n_full = t0 // tk

    def full_body(kt, carry):
        tile(pl.multiple_of(kt * tk, tk), tk, False)
        return carry

    lax.fori_loop(0, n_full, full_body, 0)
    done = n_full * tk

    def tail_body(j, carry):
        tile(pl.multiple_of(done + j * tail, tail), tail, True)
        return carry

    lax.fori_loop(0, (t0 + tq - done + tail - 1) // tail, tail_body, 0)

    band = WINDOW + tq
    s0 = pl.multiple_of(jnp.maximum(t0 - WINDOW, 0), tq)
    kw_band = kwt_ref[:, pl.ds(s0, band)]
    vw_band = vw_ref[pl.ds(s0, band), :]
    dist = tpos - (s0 + _iota((1, band), 1))
    in_window = (dist >= 0) & (dist < WINDOW)
    probs = [_softmax2(s, in_window).astype(BF16) for s in [_dot(x, kw_band) for x in q]]
    o_w = [_dot(p, vw_band) for p in probs]

    gates = _sigmoid(sm_ref[...])
    for r, sl in zip(heads, lanes):
        o_s = acc_scr[r] * (1.0 / jnp.maximum(jnp.sum(l_scr[r], axis=-1, keepdims=True), 1e-30))
        lane = SM_GATE + (g * NSA_GROUP + r) * 3
        out = (_lane_column(gates, lane) * oc_scr[r] + _lane_column(gates, lane + 1) * o_s
               + _lane_column(gates, lane + 2) * o_w[r])
        o_ref[:, sl] = out.astype(BF16)


def _attend(qn, k_cmp_t, v_cmp, overlap, ksa_t, vs, kw_t, vw, small, *, batch, seq, tq=256, tk=1024, tail=512):
    m = qn.shape[0]
    gw = NSA_GROUP * HEAD_DIM
    nq = seq // tq
    tk = min(tk, seq)
    n_cmp = v_cmp.shape[2]
    top_k = min(SEL_TOPK, seq // SEL_BLOCK)

    def tile_spec(width):
        return pl.BlockSpec((tq, width), lambda b, g, i: (b * nq + i, g))

    def seq_spec(width):
        return pl.BlockSpec((seq, width), lambda b, g, i: (b, g))

    def group_spec(rows, cols):
        return pl.BlockSpec((None, None, rows, cols), lambda b, g, i: (b, g, 0, 0))

    stat = pltpu.VMEM((NSA_GROUP, tq, LANES), F32)
    return pl.pallas_call(
        functools.partial(_attend_body, tq=tq, tk=tk, tail=tail, top_k=top_k),
        grid=(batch, NSA_KV_HEADS, nq),
        in_specs=[
            tile_spec(gw),
            group_spec(HEAD_DIM, n_cmp), group_spec(n_cmp, HEAD_DIM),
            pl.BlockSpec((n_cmp, LANES), lambda b, g, i: (0, 0)),
            group_spec(2 * HEAD_DIM, seq), seq_spec(HEAD_DIM), group_spec(HEAD_DIM, seq), seq_spec(HEAD_DIM),
            pl.BlockSpec((tq, LANES), lambda b, g, i: (b * nq + i, 0)),
        ],
        out_specs=tile_spec(gw),
        out_shape=jax.ShapeDtypeStruct((m, NSA_WIDTH), BF16),
        scratch_shapes=[stat, stat, stat, stat],
        compiler_params=_cparams(("parallel", "parallel", "arbitrary")),
        name="nsa_attend",
    )(qn, k_cmp_t, v_cmp, overlap, ksa_t, vs, kw_t, vw, small)


def _rope_tables(pos):
    half = HEAD_DIM // 2
    inv_freq = ROPE_THETA ** (-jnp.arange(half, dtype=F32) / half)
    ang = pos[:, None] * inv_freq[None, :]
    cos, sin = jnp.cos(ang), jnp.sin(ang)
    return jnp.concatenate([cos, cos], axis=-1), jnp.concatenate([-sin, sin], axis=-1)


def _overlap_matrix(n_cmp_pad, n_cmp):
    c_start = np.arange(n_cmp_pad) * CMP_STRIDE
    s_start = np.arange(LANES) * SEL_BLOCK
    ov = (c_start[:, None] < s_start[None, :] + SEL_BLOCK) & (c_start[:, None] + CMP_LEN > s_start[None, :])
    ov &= (np.arange(n_cmp_pad) < n_cmp)[:, None]
    return jnp.asarray(ov.astype(np.float32), dtype=BF16)


def _nsa(proj, small, q_gain, kc_gain, ks_gain, kw_gain, pe_k, pe_v, wk1, wk2, wv1, wv2, *, batch, seq):
    n_chunks = seq // CMP_STRIDE
    cos, sin = _rope_tables(jnp.arange(seq, dtype=F32))
    cmp_end = jnp.arange(n_chunks) * CMP_STRIDE + (CMP_LEN - 1)
    cos_c, sin_c = _rope_tables(cmp_end.astype(F32))
    qn, ksa_t, vs, kw_t, vw = _nsa_prep(proj, cos, sin, q_gain, ks_gain, kw_gain, batch=batch, seq=seq)
    k_cmp_t, v_cmp = _compress(proj, pe_k, pe_v, wk1.astype(BF16), wk2.astype(BF16), wv1.astype(BF16),
                               wv2.astype(BF16), kc_gain, cos_c, sin_c, batch=batch, seq=seq)
    overlap = _overlap_matrix(n_chunks, n_chunks - 1)
    return _attend(qn, k_cmp_t, v_cmp, overlap, ksa_t, vs, kw_t, vw, small, batch=batch, seq=seq)


def _unit_lower_inverses(mats):
    c = mats[0].shape[0]
    row = _iota((c, c), 0)
    col = _iota((c, c), 1)
    eye = (row == col).astype(F32)
    diag = row // INV_BASE == col // INV_BASE
    xs = [jnp.where(diag, -a, 0.0) for a in mats]
    ts = [eye + x for x in xs]
    x16 = [x.astype(BF16) for x in xs]
    powers = [_dot(x, x) for x in x16]
    span = 2
    while span < INV_BASE:
        p16 = [p.astype(BF16) for p in powers]
        if 2 * span < INV_BASE:
            both = [_dot(p, jnp.concatenate([p, t.astype(BF16)], axis=1)) for p, t in zip(p16, ts)]
            ts = [t + b[:, c:] for t, b in zip(ts, both)]
            powers = [b[:, :c] for b in both]
        else:
            ts = [t + _dot(p, t.astype(BF16)) for p, t in zip(p16, ts)]
        span *= 2
    size = INV_BASE
    while size < c:
        off = (row // (2 * size) == col // (2 * size)) & (row // size != col // size)
        t16 = [t.astype(BF16) for t in ts]
        inner = [_dot(jnp.where(off, a, 0.0).astype(BF16), t) for a, t in zip(mats, t16)]
        ts = [t - _dot(t2, i.astype(BF16)) for t, t2, i in zip(ts, t16, inner)]
        size *= 2
    resid = [eye - _dot_hp(_split2(eye + a), _split2(t)) for a, t in zip(mats, ts)]
    return [t + _dot(t.astype(BF16), r.astype(BF16)) for t, r in zip(ts, resid)]


def _gdn_body(xq_ref, xk_ref, xv_ref, z_ref, sm_ref, wq_ref, wk_ref, wv_ref, alog_ref, dtb_ref, og_ref, o_ref,
              xs_scr, state_scr, *, heads):
    c = GDN_CHUNK
    hb = pl.program_id(1)
    step = pl.program_id(2)
    halo = 8
    hs = range(heads)

    @pl.when(step == 0)
    def _():
        xs_scr[:, 0:halo, :] = jnp.zeros((3, halo, xs_scr.shape[2]), F32)
        state_scr[...] = jnp.zeros_like(state_scr)

    conv = []
    for idx, (x_ref, w_ref) in enumerate(((xq_ref, wq_ref), (xk_ref, wk_ref), (xv_ref, wv_ref))):
        xs_scr[idx, halo:halo + c, :] = x_ref[...]
        y = jnp.zeros(x_ref.shape, F32)
        for tap in range(GDN_CONV):
            y = y + w_ref[tap:tap + 1, :] * xs_scr[idx, pl.ds(halo - (GDN_CONV - 1) + tap, c), :]
        xs_scr[idx, 0:halo, :] = xs_scr[idx, c:c + halo, :]
        conv.append(_silu(y))
    qc, kc, vc = conv

    sm = sm_ref[...]
    beta_all = _sigmoid(sm)
    x = sm + dtb_ref[...]
    softplus = jnp.maximum(x, 0.0) + jnp.log1p(jnp.exp(-jnp.abs(x)))
    g_all = -(jnp.exp(alog_ref[...]) * softplus)
    row = _iota((c, c), 0)
    col = _iota((c, c), 1)
    tril = row >= col
    ones_tril = tril.astype(BF16)
    g_hi, g_mid, g_lo = _split3(g_all)
    gc_all = _dot(ones_tril, g_hi) + (_dot(ones_tril, g_mid) + _dot(ones_tril, g_lo))
    gc_all_t = gc_all.T

    def head_lanes(h):
        return slice(h * HEAD_DIM, (h + 1) * HEAD_DIM)

    qs, ks, gammas, g_cols, g_lasts, betas = [], [], [], [], [], []
    for h in hs:
        q = qc[:, head_lanes(h)]
        k = kc[:, head_lanes(h)]
        qs.append(q * lax.rsqrt(jnp.sum(q * q, axis=-1, keepdims=True) + NORM_EPS) * (HEAD_DIM ** -0.5))
        ks.append(k * lax.rsqrt(jnp.sum(k * k, axis=-1, keepdims=True) + NORM_EPS))
        head = hb * heads + h
        betas.append(_lane_column(beta_all, SM_BETA + head))
        g_col = _lane_column(gc_all, SM_DECAY + head)
        g_row = jnp.sum(jnp.where(_iota((LANES, c), 0) == SM_DECAY + head, gc_all_t, 0.0), axis=0, keepdims=True)
        gammas.append(jnp.exp(jnp.where(tril, g_col - g_row, -MASK_BIG)))
        g_cols.append(g_col)
        g_lasts.append(g_row[:, c - 1:c])
    decays = [jnp.exp(g) for g in g_cols]
    kbs = [k * b for k, b in zip(ks, betas)]
    k16 = [k.astype(BF16) for k in ks]
    kk_qk = [_dot_nt(jnp.concatenate([kb, q], axis=0).astype(BF16), kk) for kb, q, kk in zip(kbs, qs, k16)]
    t_inv = _unit_lower_inverses([jnp.where(row > col, x[:c] * gm, 0.0) for x, gm in zip(kk_qk, gammas)])
    uw = [_dot(t.astype(BF16), jnp.concatenate([vc[:, head_lanes(h)] * b, kb * d], axis=1).astype(BF16))
          for h, t, b, kb, d in zip(hs, t_inv, betas, kbs, decays)]
    states = [state_scr[h] for h in hs]
    ws_qs = [_dot(jnp.concatenate([u[:, HEAD_DIM:], q * d], axis=0).astype(BF16), s.astype(BF16))
             for u, q, d, s in zip(uw, qs, decays, states)]
    v_new = [(u[:, :HEAD_DIM] - w[:c]).astype(BF16) for u, w in zip(uw, ws_qs)]
    outs = [w[c:] + _dot((x[c:] * gm).astype(BF16), v) for w, x, gm, v in zip(ws_qs, kk_qk, gammas, v_new)]
    for h in hs:
        k_tail = (ks[h] * jnp.exp(g_lasts[h] - g_cols[h])).astype(BF16)
        state_scr[h] = states[h] * jnp.exp(g_lasts[h]) + _dot_tn(k_tail, v_new[h])
    for h in hs:
        o = outs[h]
        o = o * lax.rsqrt(jnp.mean(o * o, axis=-1, keepdims=True) + NORM_EPS) * og_ref[...]
        o_ref[:, head_lanes(h)] = (o * _silu(z_ref[:, head_lanes(h)])).astype(BF16)


def _lane_row(values, offset):
    return jnp.zeros((1, LANES), F32).at[0, offset:offset + values.shape[0]].set(values)


def _gdn(proj, small, conv_w, a_log, dt_bias, o_gain, *, batch, seq):
    m = proj.shape[0]
    c = GDN_CHUNK
    heads = GDN_HEAD_BLOCK
    width = heads * HEAD_DIM
    n_steps = seq // c

    def act_spec(off):
        base = off // width
        return pl.BlockSpec((c, width), lambda b, hb, s: (b * n_steps + s, base + hb))

    def conv_spec(part):
        base = part * GDN_WIDTH // width
        return pl.BlockSpec((GDN_CONV, width), lambda b, hb, s: (0, base + hb))

    row_spec = pl.BlockSpec((1, LANES), lambda b, hb, s: (0, 0))
    return pl.pallas_call(
        functools.partial(_gdn_body, heads=heads),
        grid=(batch, GDN_HEADS // heads, n_steps),
        in_specs=[
            act_spec(OFF_GQKV), act_spec(OFF_GQKV + GDN_WIDTH), act_spec(OFF_GQKV + 2 * GDN_WIDTH), act_spec(OFF_GZ),
            pl.BlockSpec((c, LANES), lambda b, hb, s: (b * n_steps + s, 0)),
            conv_spec(0), conv_spec(1), conv_spec(2),
            row_spec, row_spec, row_spec,
        ],
        out_specs=pl.BlockSpec((c, width), lambda b, hb, s: (b * n_steps + s, hb)),
        out_shape=jax.ShapeDtypeStruct((m, GDN_WIDTH), BF16),
        scratch_shapes=[pltpu.VMEM((3, c + 8, width), F32), pltpu.VMEM((heads, HEAD_DIM, HEAD_DIM), F32)],
        compiler_params=_cparams(("parallel", "parallel", "arbitrary")),
        name="gdn",
    )(proj, proj, proj, proj, small, conv_w, conv_w, conv_w,
      _lane_row(a_log, SM_DECAY), _lane_row(dt_bias, SM_DECAY), o_gain.reshape(1, -1))


def _split_w_in(w_in):
    sizes = (NSA_WIDTH, 6 * NSA_KV_WIDTH, 3 * NSA_HEADS, 3 * GDN_WIDTH, GDN_WIDTH, GDN_HEADS, GDN_HEADS, 2 * D_MODEL)
    offs = np.cumsum((0,) + sizes)
    wt = w_in.T
    nq, nkv, ngate, gqkv, gz, gbeta, gdecay, merge = (wt[offs[i]:offs[i + 1]].astype(BF16) for i in range(len(sizes)))
    big = jnp.concatenate([merge, gqkv, nq, gz, nkv], axis=0)
    pad = jnp.zeros((LANES - (3 * NSA_HEADS + 2 * GDN_HEADS), w_in.shape[0]), BF16)
    small = jnp.concatenate([ngate, gbeta, gdecay, pad], axis=0)
    return big, small


def _layer(x, p, g_mix, w_in, nsa_q_gain, nsa_kc_gain, nsa_ks_gain, nsa_kw_gain, cmp_pe_k, cmp_pe_v, cmp_wk1, cmp_wk2,
           cmp_wv1, cmp_wv2, gdn_conv_w, gdn_a_log, gdn_dt_bias, gdn_o_gain, w_up_nsa, w_up_gdn, w_out, g_mlp,
           w_mlp_in, w_mlp_out, g_ple, w_ple_gate, w_ple_proj, *, batch, seq):
    h = _rmsnorm(x, g_mix)
    w_big, w_small = _split_w_in(w_in)
    proj = _matmul_nt(h, w_big, out_dtype=F32, name="in_proj", **DENSE_TILES["in_proj"])
    small = _matmul_nt(h, w_small, out_dtype=F32, name="in_proj_small", **DENSE_TILES["in_proj_small"])
    o_a = _nsa(proj, small, nsa_q_gain, nsa_kc_gain, nsa_ks_gain, nsa_kw_gain, cmp_pe_k, cmp_pe_v, cmp_wk1, cmp_wk2,
               cmp_wv1, cmp_wv2, batch=batch, seq=seq)
    o_b = _gdn(proj, small, gdn_conv_w, gdn_a_log, gdn_dt_bias, gdn_o_gain, batch=batch, seq=seq)
    mixed = _merge(o_a, w_up_nsa.astype(BF16), o_b, w_up_gdn.astype(BF16), proj, **DENSE_TILES["merge_up"])
    x, xg, ss = _out_proj(mixed, w_out.astype(BF16), x, g_mlp, **DENSE_TILES["out_proj"])
    hidden = _mlp_in(xg, w_mlp_in.astype(BF16), ss, **DENSE_TILES["mlp_in"])
    x, xg, ss = _mlp_out(hidden, w_mlp_out.astype(BF16), x, g_ple, **DENSE_TILES["mlp_out"])
    return _ple(xg, w_ple_gate.astype(BF16), p.astype(BF16), w_ple_proj.astype(BF16), x, ss, **DENSE_TILES["ple"])


def kernel(x, p, g_mix, w_in, nsa_q_gain, nsa_kc_gain, nsa_ks_gain, nsa_kw_gain, cmp_pe_k, cmp_pe_v, cmp_wk1, cmp_wk2,
           cmp_wv1, cmp_wv2, gdn_conv_w, gdn_a_log, gdn_dt_bias, gdn_o_gain, w_up_nsa, w_up_gdn, w_out, g_mlp,
           w_mlp_in, w_mlp_out, g_ple, w_ple_gate, w_ple_proj):
    batch, seq, d = x.shape
    depth = p.shape[0]
    assert seq // SEL_BLOCK <= LANES and seq % (4 * LANES) == 0
    y = x.reshape(batch * seq, d)
    for i in range(depth):
        y = _layer(y, p[i].reshape(batch * seq, -1), g_mix[i], w_in[i], nsa_q_gain[i], nsa_kc_gain[i], nsa_ks_gain[i],
                   nsa_kw_gain[i], cmp_pe_k[i], cmp_pe_v[i], cmp_wk1[i], cmp_wk2[i], cmp_wv1[i], cmp_wv2[i],
                   gdn_conv_w[i], gdn_a_log[i], gdn_dt_bias[i], gdn_o_gain[i], w_up_nsa[i], w_up_gdn[i], w_out[i],
                   g_mlp[i], w_mlp_in[i], w_mlp_out[i], g_ple[i], w_ple_gate[i], w_ple_proj[i], batch=batch, seq=seq)
    return y.reshape(batch, seq, d)
```

```python
import functools

import numpy as np
import jax
import jax.numpy as jnp
from jax import lax
from jax.experimental import pallas as pl
from jax.experimental.pallas import tpu as pltpu

D_MODEL = 4096
HEAD_DIM = 128
NSA_HEADS = 16
NSA_KV_HEADS = 4
NSA_GROUP = NSA_HEADS // NSA_KV_HEADS
CMP_LEN = 32
CMP_STRIDE = 16
CMP_HIDDEN = 4 * HEAD_DIM
SEL_BLOCK = 64
SEL_TOPK = 16
WINDOW = 512
GDN_HEADS = 16
GDN_CONV = 4
MLP_HIDDEN = 4 * D_MODEL
PLE_DIM = 256
ROPE_THETA = 10000.0
NORM_EPS = 1e-6
MASK_BIG = 1e30
BIAS_BIG = 2.0 ** 99
LOG2_E = 1.4426950408889634

NSA_WIDTH = NSA_HEADS * HEAD_DIM
NSA_KV_WIDTH = NSA_KV_HEADS * HEAD_DIM
GDN_WIDTH = GDN_HEADS * HEAD_DIM
LANES = 128
GDN_CHUNK = 128
GDN_HEAD_BLOCK = 16
INV_BASE = 16

OFF_MERGE = 0
OFF_GQKV = OFF_MERGE + 2 * D_MODEL
OFF_NQ = OFF_GQKV + 3 * GDN_WIDTH
OFF_GZ = OFF_NQ + NSA_WIDTH
OFF_NKV = OFF_GZ + GDN_WIDTH
BIG_WIDTH = OFF_NKV + 6 * NSA_KV_WIDTH
SM_GATE = 0
SM_BETA = 3 * NSA_HEADS
SM_DECAY = SM_BETA + GDN_HEADS

V7X_VMEM_LIMIT = 56 * 1024 * 1024

DENSE_TILES = {
    "in_proj": dict(bm=1024, bn=1024),
    "in_proj_small": dict(bm=1024, bn=LANES),
    "merge_up": dict(bm=1024, bn=512),
    "out_proj": dict(bm=1024, bn=512),
    "mlp_in": dict(bm=1024, bn=1024),
    "mlp_out": dict(bm=1024, bn=1024, bk=2048),
    "ple": dict(bm=1024, bn=512),
}

F32 = jnp.float32
BF16 = jnp.bfloat16


def _cparams(sem, vmem=V7X_VMEM_LIMIT):
    return pltpu.CompilerParams(dimension_semantics=sem, vmem_limit_bytes=vmem)


def _dot(a, b):
    return jnp.dot(a, b, preferred_element_type=F32)


def _dot_nt(a, b):
    return lax.dot_general(a, b, (((1,), (1,)), ((), ())), preferred_element_type=F32)


def _dot_tn(a, b):
    return lax.dot_general(a, b, (((0,), (0,)), ((), ())), preferred_element_type=F32)


def _split3(x):
    hi = x.astype(BF16)
    r = x - hi.astype(F32)
    mid = r.astype(BF16)
    lo = (r - mid.astype(F32)).astype(BF16)
    return hi, mid, lo


def _split2(x):
    hi = x.astype(BF16)
    lo = (x - hi.astype(F32)).astype(BF16)
    return hi, lo


def _dot_hp(a, b):
    return _dot(a[0], b[0]) + (_dot(a[0], b[1]) + _dot(a[1], b[0]))


def _sigmoid(x):
    return 0.5 * jnp.tanh(0.5 * x) + 0.5


def _silu(x):
    h = 0.5 * x
    return h * jnp.tanh(h) + h


def _iota(shape, dim):
    return lax.broadcasted_iota(jnp.int32, shape, dim)


def _rmsnorm_body(x_ref, g_ref, o_ref):
    x = x_ref[...]
    y = x * lax.rsqrt(jnp.mean(x * x, axis=-1, keepdims=True) + NORM_EPS)
    o_ref[...] = (y * g_ref[...]).astype(o_ref.dtype)


def _rmsnorm(x, gain, *, rows=256):
    m, d = x.shape
    return pl.pallas_call(
        _rmsnorm_body,
        grid=(m // rows,),
        in_specs=[pl.BlockSpec((rows, d), lambda i: (i, 0)), pl.BlockSpec((1, d), lambda i: (0, 0))],
        out_specs=pl.BlockSpec((rows, d), lambda i: (i, 0)),
        out_shape=jax.ShapeDtypeStruct((m, d), BF16),
        compiler_params=_cparams(("parallel",)),
        name="rmsnorm",
    )(x, gain.reshape(1, d))


def _row_spec(bm, width):
    return pl.BlockSpec((bm, width), lambda i, j: (i, 0))


def _mm_nt_body(a_ref, bt_ref, o_ref):
    o_ref[...] = _dot_nt(a_ref[...], bt_ref[...]).astype(o_ref.dtype)


def _matmul_nt(a, b_t, *, bm, bn, out_dtype, name):
    m, kdim = a.shape
    n = b_t.shape[0]
    return pl.pallas_call(
        _mm_nt_body,
        grid=(m // bm, n // bn),
        in_specs=[_row_spec(bm, kdim), pl.BlockSpec((bn, kdim), lambda i, j: (j, 0))],
        out_specs=pl.BlockSpec((bm, bn), lambda i, j: (i, j)),
        out_shape=jax.ShapeDtypeStruct((m, n), out_dtype),
        compiler_params=_cparams(("parallel", "parallel")),
        name=name,
    )(a, b_t)


def _mm_nt_rows_body(rows_ref, a_ref, bt_ref, o_ref):
    del rows_ref
    o_ref[...] = _dot_nt(a_ref[...], bt_ref[...]).astype(o_ref.dtype)


def _matmul_nt_rows(a, b_t, row_starts, *, bm, bn, out_dtype, name):
    m, kdim = a.shape
    n_col = len(row_starts)
    align = 16
    assert all(r % align == 0 and r + bn <= b_t.shape[0] for r in row_starts)
    grid_spec = pltpu.PrefetchScalarGridSpec(
        num_scalar_prefetch=1,
        grid=(m // bm, n_col),
        in_specs=[pl.BlockSpec((bm, kdim), lambda i, j, rows: (i, 0)),
                  pl.BlockSpec((pl.Element(bn), pl.Element(kdim)), lambda i, j, rows: (rows[j] * align, 0))],
        out_specs=pl.BlockSpec((bm, bn), lambda i, j, rows: (i, j)),
    )
    return pl.pallas_call(
        _mm_nt_rows_body,
        grid_spec=grid_spec,
        out_shape=jax.ShapeDtypeStruct((m, n_col * bn), out_dtype),
        compiler_params=_cparams(("parallel", "parallel")),
        name=name,
    )(jnp.asarray([r // align for r in row_starts], jnp.int32), a, b_t)


def _norm_outputs(x, g_ref, xg_ref, ss_ref):
    xg_ref[...] = (x * g_ref[...]).astype(BF16)
    sq = x * x
    ss_ref[...] = functools.reduce(jnp.add, [sq[:, c * LANES:(c + 1) * LANES] for c in range(sq.shape[1] // LANES)])


def _row_scale(ss_ref, width):
    return lax.rsqrt(jnp.sum(ss_ref[...], axis=-1, keepdims=True) * (1.0 / width) + NORM_EPS)


def _norm_out_shapes(m, n, bn):
    return [jax.ShapeDtypeStruct((m, n), F32), jax.ShapeDtypeStruct((m, n), BF16),
            jax.ShapeDtypeStruct((m, n // bn * LANES), F32)]


def _out_proj_body(a_ref, b_ref, x_ref, g_ref, o_ref, xg_ref, ss_ref):
    x = x_ref[...] + _dot(a_ref[...], b_ref[...])
    o_ref[...] = x
    _norm_outputs(x, g_ref, xg_ref, ss_ref)


def _out_proj(a, b, x, gain, *, bm, bn):
    m, kdim = a.shape
    n = b.shape[1]
    tile = pl.BlockSpec((bm, bn), lambda i, j: (i, j))
    return pl.pallas_call(
        _out_proj_body,
        grid=(m // bm, n // bn),
        in_specs=[_row_spec(bm, kdim), pl.BlockSpec((kdim, bn), lambda i, j: (0, j)),
                  tile, pl.BlockSpec((1, bn), lambda i, j: (0, j))],
        out_specs=[tile, tile, pl.BlockSpec((bm, LANES), lambda i, j: (i, j))],
        out_shape=_norm_out_shapes(m, n, bn),
        compiler_params=_cparams(("parallel", "parallel")),
        name="out_proj",
    )(a, b, x, gain.reshape(1, n))


def _mlp_in_body(a_ref, b_ref, ss_ref, o_ref, *, width):
    r = jnp.maximum(_dot(a_ref[...], b_ref[...]) * _row_scale(ss_ref, width), 0.0)
    o_ref[...] = (r * r).astype(o_ref.dtype)


def _mlp_in(xg, b, ss, *, bm, bn):
    m, kdim = xg.shape
    n = b.shape[1]
    return pl.pallas_call(
        functools.partial(_mlp_in_body, width=kdim),
        grid=(m // bm, n // bn),
        in_specs=[pl.BlockSpec((bm, kdim), lambda i, j: (i, 0)), pl.BlockSpec((kdim, bn), lambda i, j: (0, j)),
                  pl.BlockSpec((bm, ss.shape[1]), lambda i, j: (i, 0))],
        out_specs=pl.BlockSpec((bm, bn), lambda i, j: (i, j)),
        out_shape=jax.ShapeDtypeStruct((m, n), BF16),
        compiler_params=_cparams(("parallel", "parallel")),
        name="mlp_in",
    )(xg, b, ss)


def _mlp_out_body(a_ref, b_ref, res_ref, g_ref, o_ref, xg_ref, ss_ref, *, nk):
    k = pl.program_id(2)

    @pl.when(k == 0)
    def _():
        o_ref[...] = res_ref[...]

    o_ref[...] += _dot(a_ref[...], b_ref[...])

    @pl.when(k == nk - 1)
    def _():
        _norm_outputs(o_ref[...], g_ref, xg_ref, ss_ref)


def _mlp_out(a, b, res, gain, *, bm, bn, bk):
    m, kdim = a.shape
    n = b.shape[1]
    nk = kdim // bk
    tile = pl.BlockSpec((bm, bn), lambda i, j, k: (i, j))
    return pl.pallas_call(
        functools.partial(_mlp_out_body, nk=nk),
        grid=(m // bm, n // bn, nk),
        in_specs=[pl.BlockSpec((bm, bk), lambda i, j, k: (i, k)), pl.BlockSpec((bk, bn), lambda i, j, k: (k, j)),
                  tile, pl.BlockSpec((1, bn), lambda i, j, k: (0, j))],
        out_specs=[tile, tile, pl.BlockSpec((bm, LANES), lambda i, j, k: (i, j))],
        out_shape=_norm_out_shapes(m, n, bn),
        compiler_params=_cparams(("parallel", "parallel", "arbitrary")),
        name="mlp_out",
    )(a, b, res, gain.reshape(1, n))


def _merge_body(oa_ref, wa_ref, ob_ref, wb_ref, ga_ref, gb_ref, o_ref):
    ya = _dot(oa_ref[...], wa_ref[...])
    yb = _dot(ob_ref[...], wb_ref[...])
    o_ref[...] = (_sigmoid(ga_ref[...]) * ya + _sigmoid(gb_ref[...]) * yb).astype(o_ref.dtype)


def _merge(o_a, w_a, o_b, w_b, proj, *, bm, bn):
    m, ka = o_a.shape
    kb = o_b.shape[1]
    n = w_a.shape[1]
    ga_blk = OFF_MERGE // bn
    gb_blk = (OFF_MERGE + D_MODEL) // bn
    return pl.pallas_call(
        _merge_body,
        grid=(m // bm, n // bn),
        in_specs=[
            _row_spec(bm, ka),
            pl.BlockSpec((ka, bn), lambda i, j: (0, j)),
            _row_spec(bm, kb),
            pl.BlockSpec((kb, bn), lambda i, j: (0, j)),
            pl.BlockSpec((bm, bn), lambda i, j: (i, ga_blk + j)),
            pl.BlockSpec((bm, bn), lambda i, j: (i, gb_blk + j)),
        ],
        out_specs=pl.BlockSpec((bm, bn), lambda i, j: (i, j)),
        out_shape=jax.ShapeDtypeStruct((m, n), BF16),
        compiler_params=_cparams(("parallel", "parallel")),
        name="merge_up",
    )(o_a, w_a, o_b, w_b, proj, proj)


def _ple_body(xg_ref, wg_ref, p_ref, wp_ref, x_ref, ss_ref, o_ref, *, width):
    gate = _sigmoid(_dot(xg_ref[...], wg_ref[...]) * _row_scale(ss_ref, width))
    o_ref[...] = x_ref[...] + gate * _dot(p_ref[...], wp_ref[...])


def _ple(xg, w_gate, p, w_proj, x, ss, *, bm, bn):
    m, kd = xg.shape
    kp = p.shape[1]
    n = w_gate.shape[1]
    return pl.pallas_call(
        functools.partial(_ple_body, width=kd),
        grid=(m // bm, n // bn),
        in_specs=[
            _row_spec(bm, kd),
            pl.BlockSpec((kd, bn), lambda i, j: (0, j)),
            _row_spec(bm, kp),
            pl.BlockSpec((kp, bn), lambda i, j: (0, j)),
            pl.BlockSpec((bm, bn), lambda i, j: (i, j)),
            _row_spec(bm, ss.shape[1]),
        ],
        out_specs=pl.BlockSpec((bm, bn), lambda i, j: (i, j)),
        out_shape=jax.ShapeDtypeStruct((m, n), F32),
        compiler_params=_cparams(("parallel", "parallel")),
        name="ple",
    )(xg, w_gate, p, w_proj, x, ss)


def _norm_rope(x, gain, cos, sin_signed):
    y = x * lax.rsqrt(jnp.mean(x * x, axis=-1, keepdims=True) + NORM_EPS) * gain
    return y * cos + pltpu.roll(y, HEAD_DIM // 2, 1) * sin_signed


def _nsa_prep_body(q_ref, ks_ref, vs_ref, kw_ref, vw_ref, cos_ref, sin_ref, qg_ref, ksg_ref, kwg_ref,
                   qo_ref, ksat_ref, vso_ref, kwt_ref, vwo_ref, *, rows, blocks_per_seq):
    cos = cos_ref[...]
    sin = sin_ref[...]
    q_scale = (HEAD_DIM ** -0.5) * LOG2_E
    lanes = [slice(h * HEAD_DIM, (h + 1) * HEAD_DIM) for h in range(NSA_HEADS)]
    kv_lanes = lanes[:NSA_KV_HEADS]
    heads = ([(q_ref[:, sl], qg_ref) for sl in lanes] + [(ks_ref[:, sl], ksg_ref) for sl in kv_lanes]
             + [(kw_ref[:, sl], kwg_ref) for sl in kv_lanes])
    inv = [lax.rsqrt(jnp.mean(x * x, axis=-1, keepdims=True) + NORM_EPS) for x, _ in heads]
    normed = [x * r * g_ref[...] for (x, g_ref), r in zip(heads, inv)]
    turned = [pltpu.roll(y, HEAD_DIM // 2, 1) for y in normed]
    roped = [y * cos + z * sin for y, z in zip(normed, turned)]
    for sl, y in zip(lanes, roped[:NSA_HEADS]):
        qo_ref[:, sl] = (y * q_scale).astype(BF16)
    t = (pl.program_id(0) % blocks_per_seq) * rows + _iota((LANES, rows), 1)
    onehot_t = (_iota((LANES, rows), 0) == t // SEL_BLOCK).astype(BF16)
    for g in range(NSA_KV_HEADS):
        ksat_ref[g, 0:HEAD_DIM, :] = roped[NSA_HEADS + g].T.astype(BF16)
        ksat_ref[g, HEAD_DIM:2 * HEAD_DIM, :] = onehot_t
        kwt_ref[g] = roped[NSA_HEADS + NSA_KV_HEADS + g].T.astype(BF16)
    vso_ref[...] = vs_ref[...].astype(BF16)
    vwo_ref[...] = vw_ref[...].astype(BF16)


def _nsa_prep(proj, cos, sin_signed, q_gain, ks_gain, kw_gain, *, batch, seq, rows=256):
    m = proj.shape[0]
    bps = seq // rows
    kvw = NSA_KV_WIDTH
    kv_blk = OFF_NKV // kvw

    def col(c):
        return lambda i: (i, c)

    tab = pl.BlockSpec((rows, HEAD_DIM), lambda i: (i % bps, 0))
    gain = pl.BlockSpec((1, HEAD_DIM), lambda i: (0, 0))
    return pl.pallas_call(
        functools.partial(_nsa_prep_body, rows=rows, blocks_per_seq=bps),
        grid=(m // rows,),
        in_specs=[
            pl.BlockSpec((rows, NSA_WIDTH), col(OFF_NQ // NSA_WIDTH)),
            pl.BlockSpec((rows, kvw), col(kv_blk + 2)),
            pl.BlockSpec((rows, kvw), col(kv_blk + 3)),
            pl.BlockSpec((rows, kvw), col(kv_blk + 4)),
            pl.BlockSpec((rows, kvw), col(kv_blk + 5)),
            tab, tab, gain, gain, gain,
        ],
        out_specs=[
            pl.BlockSpec((rows, NSA_WIDTH), col(0)),
            pl.BlockSpec((None, NSA_KV_HEADS, 2 * HEAD_DIM, rows), lambda i: (i // bps, 0, 0, i % bps)),
            pl.BlockSpec((rows, kvw), col(0)),
            pl.BlockSpec((None, NSA_KV_HEADS, HEAD_DIM, rows), lambda i: (i // bps, 0, 0, i % bps)),
            pl.BlockSpec((rows, kvw), col(0)),
        ],
        out_shape=[
            jax.ShapeDtypeStruct((m, NSA_WIDTH), BF16),
            jax.ShapeDtypeStruct((batch, NSA_KV_HEADS, 2 * HEAD_DIM, seq), BF16),
            jax.ShapeDtypeStruct((m, kvw), BF16),
            jax.ShapeDtypeStruct((batch, NSA_KV_HEADS, HEAD_DIM, seq), BF16),
            jax.ShapeDtypeStruct((m, kvw), BF16),
        ],
        compiler_params=_cparams(("parallel",)),
        name="nsa_prep",
    )(proj, proj, proj, proj, proj, cos, sin_signed, q_gain.reshape(1, -1), ks_gain.reshape(1, -1), kw_gain.reshape(1, -1))


def _gelu_tanh(x):
    return 0.5 * x * (1.0 + jnp.tanh(np.sqrt(2.0 / np.pi).astype(np.float32) * (x + 0.044715 * (x * x * x))))


def _compress_one(x_ref, pe_ref, w1_ref, w2_ref, n_chunks):
    half = CMP_LEN // 2
    first = jnp.zeros((n_chunks, CMP_HIDDEN), F32)
    second = jnp.zeros((n_chunks, CMP_HIDDEN), F32)
    for l in range(half):
        xl = x_ref[pl.ds(l, n_chunks, stride=CMP_STRIDE), :]
        first += _dot((xl + pe_ref[l:l + 1, :]).astype(BF16), w1_ref[l])
        second += _dot((xl + pe_ref[half + l:half + l + 1, :]).astype(BF16), w1_ref[half + l])
    hid = _gelu_tanh(first + pltpu.roll(second, n_chunks - 1, 0))
    return _dot(hid.astype(BF16), w2_ref[...])


def _compress_body(xk_ref, xv_ref, pek_ref, pev_ref, wk1_ref, wk2_ref, wv1_ref, wv2_ref, kg_ref, cos_ref, sin_ref,
                   kto_ref, vo_ref, *, n_chunks):
    k = _compress_one(xk_ref, pek_ref, wk1_ref, wk2_ref, n_chunks)
    kto_ref[...] = _norm_rope(k, kg_ref[...], cos_ref[...], sin_ref[...]).T.astype(BF16)
    vo_ref[...] = _compress_one(xv_ref, pev_ref, wv1_ref, wv2_ref, n_chunks).astype(BF16)


def _compress(proj, pe_k, pe_v, wk1, wk2, wv1, wv2, kc_gain, cos_c, sin_c, *, batch, seq):
    n_chunks = seq // CMP_STRIDE
    kc_blk = OFF_NKV // HEAD_DIM
    vc_blk = (OFF_NKV + NSA_KV_WIDTH) // HEAD_DIM

    def full(shape):
        return pl.BlockSpec(shape, lambda b, g: (0,) * len(shape))

    return pl.pallas_call(
        functools.partial(_compress_body, n_chunks=n_chunks),
        grid=(batch, NSA_KV_HEADS),
        in_specs=[
            pl.BlockSpec((seq, HEAD_DIM), lambda b, g: (b, kc_blk + g)),
            pl.BlockSpec((seq, HEAD_DIM), lambda b, g: (b, vc_blk + g)),
            full((CMP_LEN, HEAD_DIM)), full((CMP_LEN, HEAD_DIM)),
            full((CMP_LEN, HEAD_DIM, CMP_HIDDEN)), full((CMP_HIDDEN, HEAD_DIM)),
            full((CMP_LEN, HEAD_DIM, CMP_HIDDEN)), full((CMP_HIDDEN, HEAD_DIM)),
            full((1, HEAD_DIM)), full((n_chunks, HEAD_DIM)), full((n_chunks, HEAD_DIM)),
        ],
        out_specs=[
            pl.BlockSpec((None, None, HEAD_DIM, n_chunks), lambda b, g: (b, g, 0, 0)),
            pl.BlockSpec((None, None, n_chunks, HEAD_DIM), lambda b, g: (b, g, 0, 0)),
        ],
        out_shape=[
            jax.ShapeDtypeStruct((batch, NSA_KV_HEADS, HEAD_DIM, n_chunks), BF16),
            jax.ShapeDtypeStruct((batch, NSA_KV_HEADS, n_chunks, HEAD_DIM), BF16),
        ],
        compiler_params=_cparams(("parallel", "parallel")),
        name="nsa_compress",
    )(proj, proj, pe_k, pe_v, wk1, wk2, wv1, wv2, kc_gain.reshape(1, -1), cos_c, sin_c)


def _softmax2(s, mask):
    s = jnp.where(mask, s, -MASK_BIG)
    e = jnp.where(mask, jnp.exp2(s - jnp.max(s, axis=-1, keepdims=True)), 0.0)
    return e * (1.0 / jnp.maximum(jnp.sum(e, axis=-1, keepdims=True), 1e-30))


def _select_blocks(imp, t0, top_k):
    tq = imp.shape[0]
    imp_t = imp.T
    n_lane = imp_t.shape[0]
    j = _iota((n_lane, tq), 0)
    cur = (t0 + _iota((n_lane, tq), 1)) // SEL_BLOCK
    valid = j <= cur
    forced = (j == 0) | (j == cur) | (j == cur - 1)
    score = jnp.where(valid & jnp.logical_not(forced), imp_t, -jnp.inf)
    for _ in range(top_k - 3):
        best = jnp.max(score, axis=0, keepdims=True)
        first = jnp.min(jnp.where(score == best, j, n_lane), axis=0, keepdims=True)
        score = jnp.where(j == first, -jnp.inf, score)
    sel = jnp.where(valid & (score == -jnp.inf), 0.0, -BIAS_BIG)
    return sel.T.astype(BF16)


def _lane_column(x, lane):
    return jnp.sum(jnp.where(_iota(x.shape, 1) == lane, x, 0.0), axis=1, keepdims=True)


def _flash_scores(s, m_ref, l_ref, r):
    chunks = [s[:, c * LANES:(c + 1) * LANES] for c in range(s.shape[1] // LANES)]
    m_prev = m_ref[r]
    row_max = jnp.max(functools.reduce(jnp.maximum, chunks), axis=-1, keepdims=True)
    m_new = jnp.maximum(m_prev, jnp.broadcast_to(row_max, m_prev.shape))
    alpha = jnp.exp2(m_prev - m_new)
    ps = [jnp.exp2(ch - m_new) for ch in chunks]
    l_ref[r] = alpha * l_ref[r] + functools.reduce(jnp.add, ps)
    m_ref[r] = m_new
    return jnp.concatenate([x.astype(BF16) for x in ps], axis=1), alpha


def _attend_body(q_ref, kct_ref, vc_ref, ov_ref, ksat_ref, vs_ref, kwt_ref, vw_ref, sm_ref, o_ref,
                 m_scr, l_scr, acc_scr, oc_scr, *, tq, tk, tail, top_k):
    g = pl.program_id(1)
    t0 = pl.program_id(2) * tq
    tpos = t0 + _iota((tq, 1), 0)
    heads = range(NSA_GROUP)
    lanes = [slice(r * HEAD_DIM, (r + 1) * HEAD_DIM) for r in heads]
    q = [q_ref[:, sl] for sl in lanes]

    n_cmp = kct_ref.shape[1]
    visible = _iota((1, n_cmp), 1) * CMP_STRIDE + (CMP_LEN - 1) <= tpos
    kct = kct_ref[...]
    vc = vc_ref[...]
    probs = [_softmax2(s, visible) for s in [_dot(x, kct) for x in q]]
    for r, p in zip(heads, probs):
        oc_scr[r] = _dot(p.astype(BF16), vc)
    ov = ov_ref[...]
    p_hi, p_mid, p_lo = _split3(functools.reduce(jnp.add, probs))
    imp = _dot(p_hi, ov) + (_dot(p_mid, ov) + _dot(p_lo, ov))
    sb = _select_blocks(imp, t0, top_k)

    qa = [jnp.concatenate([x, sb], axis=1) for x in q]
    m_scr[...] = jnp.full(m_scr.shape, -MASK_BIG, F32)
    l_scr[...] = jnp.zeros(l_scr.shape, F32)
    acc_scr[...] = jnp.zeros(acc_scr.shape, F32)

    def tile(k0, width, causal):
        k_tile = ksat_ref[:, pl.ds(k0, width)]
        v_tile = vs_ref[pl.ds(k0, width), :]
        scores = [_dot(x, k_tile) for x in qa]
        if causal:
            keep = k0 + _iota((1, width), 1) <= tpos
            scores = [jnp.where(keep, s, -MASK_BIG) for s in scores]
        stats = [_flash_scores(scores[r], m_scr, l_scr, r) for r in heads]
        for r, (p, alpha) in zip(heads, stats):
            acc_scr[r] = alpha * acc_scr[r] + _dot(p, v_tile)

    n_full = t0 // tk

    def full_body(kt, carry):
        tile(pl.multiple_of(kt * tk, tk), tk, False)
        return carry

    lax.fori_loop(0, n_full, full_body, 0)
    done = n_full * tk

    def tail_body(j, carry):
        tile(pl.multiple_of(done + j * tail, tail), tail, True)
        return carry

    lax.fori_loop(0, (t0 + tq - done + tail - 1) // tail, tail_body, 0)

    band = WINDOW + tq
    s0 = pl.multiple_of(jnp.maximum(t0 - WINDOW, 0), tq)
    kw_band = kwt_ref[:, pl.ds(s0, band)]
    vw_band = vw_ref[pl.ds(s0, band), :]
    dist = tpos - (s0 + _iota((1, band), 1))
    in_window = (dist >= 0) & (dist < WINDOW)
    probs = [_softmax2(s, in_window).astype(BF16) for s in [_dot(x, kw_band) for x in q]]
    o_w = [_dot(p, vw_band) for p in probs]

    gates = _sigmoid(sm_ref[...])
    for r, sl in zip(heads, lanes):
        o_s = acc_scr[r] * (1.0 / jnp.maximum(jnp.sum(l_scr[r], axis=-1, keepdims=True), 1e-30))
        lane = SM_GATE + (g * NSA_GROUP + r) * 3
        out = (_lane_column(gates, lane) * oc_scr[r] + _lane_column(gates, lane + 1) * o_s
               + _lane_column(gates, lane + 2) * o_w[r])
        o_ref[:, sl] = out.astype(BF16)


def _attend(qn, k_cmp_t, v_cmp, overlap, ksa_t, vs, kw_t, vw, small, *, batch, seq, tq=256, tk=1024, tail=512):
    m = qn.shape[0]
    gw = NSA_GROUP * HEAD_DIM
    nq = seq // tq
    tk = min(tk, seq)
    n_cmp = v_cmp.shape[2]
    top_k = min(SEL_TOPK, seq // SEL_BLOCK)

    def tile_spec(width):
        return pl.BlockSpec((tq, width), lambda b, g, i: (b * nq + i, g))

    def seq_spec(width):
        return pl.BlockSpec((seq, width), lambda b, g, i: (b, g))

    def group_spec(rows, cols):
        return pl.BlockSpec((None, None, rows, cols), lambda b, g, i: (b, g, 0, 0))

    stat = pltpu.VMEM((NSA_GROUP, tq, LANES), F32)
    return pl.pallas_call(
        functools.partial(_attend_body, tq=tq, tk=tk, tail=tail, top_k=top_k),
        grid=(batch, NSA_KV_HEADS, nq),
        in_specs=[
            tile_spec(gw),
            group_spec(HEAD_DIM, n_cmp), group_spec(n_cmp, HEAD_DIM),
            pl.BlockSpec((n_cmp, LANES), lambda b, g, i: (0, 0)),
            group_spec(2 * HEAD_DIM, seq), seq_spec(HEAD_DIM), group_spec(HEAD_DIM, seq), seq_spec(HEAD_DIM),
            pl.BlockSpec((tq, LANES), lambda b, g, i: (b * nq + i, 0)),
        ],
        out_specs=tile_spec(gw),
        out_shape=jax.ShapeDtypeStruct((m, NSA_WIDTH), BF16),
        scratch_shapes=[stat, stat, stat, stat],
        compiler_params=_cparams(("parallel", "parallel", "arbitrary")),
        name="nsa_attend",
    )(qn, k_cmp_t, v_cmp, overlap, ksa_t, vs, kw_t, vw, small)


def _rope_tables(pos):
    half = HEAD_DIM // 2
    inv_freq = ROPE_THETA ** (-jnp.arange(half, dtype=F32) / half)
    ang = pos[:, None] * inv_freq[None, :]
    cos, sin = jnp.cos(ang), jnp.sin(ang)
    return jnp.concatenate([cos, cos], axis=-1), jnp.concatenate([-sin, sin], axis=-1)


def _overlap_matrix(n_cmp_pad, n_cmp):
    c_start = np.arange(n_cmp_pad) * CMP_STRIDE
    s_start = np.arange(LANES) * SEL_BLOCK
    ov = (c_start[:, None] < s_start[None, :] + SEL_BLOCK) & (c_start[:, None] + CMP_LEN > s_start[None, :])
    ov &= (np.arange(n_cmp_pad) < n_cmp)[:, None]
    return jnp.asarray(ov.astype(np.float32), dtype=BF16)


def _nsa(proj, small, q_gain, kc_gain, ks_gain, kw_gain, pe_k, pe_v, wk1, wk2, wv1, wv2, *, batch, seq):
    n_chunks = seq // CMP_STRIDE
    cos, sin = _rope_tables(jnp.arange(seq, dtype=F32))
    cmp_end = jnp.arange(n_chunks) * CMP_STRIDE + (CMP_LEN - 1)
    cos_c, sin_c = _rope_tables(cmp_end.astype(F32))
    qn, ksa_t, vs, kw_t, vw = _nsa_prep(proj, cos, sin, q_gain, ks_gain, kw_gain, batch=batch, seq=seq)
    k_cmp_t, v_cmp = _compress(proj, pe_k, pe_v, wk1.astype(BF16), wk2.astype(BF16), wv1.astype(BF16),
                               wv2.astype(BF16), kc_gain, cos_c, sin_c, batch=batch, seq=seq)
    overlap = _overlap_matrix(n_chunks, n_chunks - 1)
    return _attend(qn, k_cmp_t, v_cmp, overlap, ksa_t, vs, kw_t, vw, small, batch=batch, seq=seq)


def _unit_lower_inverses(mats):
    c = mats[0].shape[0]
    row = _iota((c, c), 0)
    col = _iota((c, c), 1)
    eye = (row == col).astype(F32)
    diag = row // INV_BASE == col // INV_BASE
    xs = [jnp.where(diag, -a, 0.0) for a in mats]
    ts = [eye + x for x in xs]
    x16 = [x.astype(BF16) for x in xs]
    powers = [_dot(x, x) for x in x16]
    span = 2
    while span < INV_BASE:
        p16 = [p.astype(BF16) for p in powers]
        if 2 * span < INV_BASE:
            both = [_dot(p, jnp.concatenate([p, t.astype(BF16)], axis=1)) for p, t in zip(p16, ts)]
            ts = [t + b[:, c:] for t, b in zip(ts, both)]
            powers = [b[:, :c] for b in both]
        else:
            ts = [t + _dot(p, t.astype(BF16)) for p, t in zip(p16, ts)]
        span *= 2
    size = INV_BASE
    while size < c:
        off = (row // (2 * size) == col // (2 * size)) & (row // size != col // size)
        t16 = [t.astype(BF16) for t in ts]
        inner = [_dot(jnp.where(off, a, 0.0).astype(BF16), t) for a, t in zip(mats, t16)]
        ts = [t - _dot(t2, i.astype(BF16)) for t, t2, i in zip(ts, t16, inner)]
        size *= 2
    resid = [eye - _dot_hp(_split2(eye + a), _split2(t)) for a, t in zip(mats, ts)]
    return [t + _dot(t.astype(BF16), r.astype(BF16)) for t, r in zip(ts, resid)]


def _gdn_body(xq_ref, xk_ref, xv_ref, z_ref, sm_ref, wq_ref, wk_ref, wv_ref, alog_ref, dtb_ref, og_ref, o_ref,
              xs_scr, state_scr, *, heads):
    c = GDN_CHUNK
    hb = pl.program_id(1)
    step = pl.program_id(2)
    halo = 8
    hs = range(heads)

    @pl.when(step == 0)
    def _():
        xs_scr[:, 0:halo, :] = jnp.zeros((3, halo, xs_scr.shape[2]), F32)
        state_scr[...] = jnp.zeros_like(state_scr)

    conv = []
    for idx, (x_ref, w_ref) in enumerate(((xq_ref, wq_ref), (xk_ref, wk_ref), (xv_ref, wv_ref))):
        xs_scr[idx, halo:halo + c, :] = x_ref[...]
        y = jnp.zeros(x_ref.shape, F32)
        for tap in range(GDN_CONV):
            y = y + w_ref[tap:tap + 1, :] * xs_scr[idx, pl.ds(halo - (GDN_CONV - 1) + tap, c), :]
        xs_scr[idx, 0:halo, :] = xs_scr[idx, c:c + halo, :]
        conv.append(_silu(y))
    qc, kc, vc = conv

    sm = sm_ref[...]
    beta_all = _sigmoid(sm)
    x = sm + dtb_ref[...]
    softplus = jnp.maximum(x, 0.0) + jnp.log1p(jnp.exp(-jnp.abs(x)))
    g_all = -(jnp.exp(alog_ref[...]) * softplus)
    row = _iota((c, c), 0)
    col = _iota((c, c), 1)
    tril = row >= col
    ones_tril = tril.astype(BF16)
    g_hi, g_mid, g_lo = _split3(g_all)
    gc_all = _dot(ones_tril, g_hi) + (_dot(ones_tril, g_mid) + _dot(ones_tril, g_lo))
    gc_all_t = gc_all.T

    def head_lanes(h):
        return slice(h * HEAD_DIM, (h + 1) * HEAD_DIM)

    qs, ks, gammas, g_cols, g_lasts, betas = [], [], [], [], [], []
    for h in hs:
        q = qc[:, head_lanes(h)]
        k = kc[:, head_lanes(h)]
        qs.append(q * lax.rsqrt(jnp.sum(q * q, axis=-1, keepdims=True) + NORM_EPS) * (HEAD_DIM ** -0.5))
        ks.append(k * lax.rsqrt(jnp.sum(k * k, axis=-1, keepdims=True) + NORM_EPS))
        head = hb * heads + h
        betas.append(_lane_column(beta_all, SM_BETA + head))
        g_col = _lane_column(gc_all, SM_DECAY + head)
        g_row = jnp.sum(jnp.where(_iota((LANES, c), 0) == SM_DECAY + head, gc_all_t, 0.0), axis=0, keepdims=True)
        gammas.append(jnp.exp(jnp.where(tril, g_col - g_row, -MASK_BIG)))
        g_cols.append(g_col)
        g_lasts.append(g_row[:, c - 1:c])
    decays = [jnp.exp(g) for g in g_cols]
    kbs = [k * b for k, b in zip(ks, betas)]
    k16 = [k.astype(BF16) for k in ks]
    kk_qk = [_dot_nt(jnp.concatenate([kb, q], axis=0).astype(BF16), kk) for kb, q, kk in zip(kbs, qs, k16)]
    t_inv = _unit_lower_inverses([jnp.where(row > col, x[:c] * gm, 0.0) for x, gm in zip(kk_qk, gammas)])
    uw = [_dot(t.astype(BF16), jnp.concatenate([vc[:, head_lanes(h)] * b, kb * d], axis=1).astype(BF16))
          for h, t, b, kb, d in zip(hs, t_inv, betas, kbs, decays)]
    states = [state_scr[h] for h in hs]
    ws_qs = [_dot(jnp.concatenate([u[:, HEAD_DIM:], q * d], axis=0).astype(BF16), s.astype(BF16))
             for u, q, d, s in zip(uw, qs, decays, states)]
    v_new = [(u[:, :HEAD_DIM] - w[:c]).astype(BF16) for u, w in zip(uw, ws_qs)]
    outs = [w[c:] + _dot((x[c:] * gm).astype(BF16), v) for w, x, gm, v in zip(ws_qs, kk_qk, gammas, v_new)]
    for h in hs:
        k_tail = (ks[h] * jnp.exp(g_lasts[h] - g_cols[h])).astype(BF16)
        state_scr[h] = states[h] * jnp.exp(g_lasts[h]) + _dot_tn(k_tail, v_new[h])
    for h in hs:
        o = outs[h]
        o = o * lax.rsqrt(jnp.mean(o * o, axis=-1, keepdims=True) + NORM_EPS) * og_ref[...]
        o_ref[:, head_lanes(h)] = (o * _silu(z_ref[:, head_lanes(h)])).astype(BF16)


def _lane_row(values, offset):
    return jnp.zeros((1, LANES), F32).at[0, offset:offset + values.shape[0]].set(values)


def _gdn(proj, small, conv_w, a_log, dt_bias, o_gain, *, batch, seq):
    m = proj.shape[0]
    c = GDN_CHUNK
    heads = GDN_HEAD_BLOCK
    width = heads * HEAD_DIM
    n_steps = seq // c

    def act_spec(off):
        base = off // width
        return pl.BlockSpec((c, width), lambda b, hb, s: (b * n_steps + s, base + hb))

    def conv_spec(part):
        base = part * GDN_WIDTH // width
        return pl.BlockSpec((GDN_CONV, width), lambda b, hb, s: (0, base + hb))

    row_spec = pl.BlockSpec((1, LANES), lambda b, hb, s: (0, 0))
    return pl.pallas_call(
        functools.partial(_gdn_body, heads=heads),
        grid=(batch, GDN_HEADS // heads, n_steps),
        in_specs=[
            act_spec(OFF_GQKV), act_spec(OFF_GQKV + GDN_WIDTH), act_spec(OFF_GQKV + 2 * GDN_WIDTH), act_spec(OFF_GZ),
            pl.BlockSpec((c, LANES), lambda b, hb, s: (b * n_steps + s, 0)),
            conv_spec(0), conv_spec(1), conv_spec(2),
            row_spec, row_spec, row_spec,
        ],
        out_specs=pl.BlockSpec((c, width), lambda b, hb, s: (b * n_steps + s, hb)),
        out_shape=jax.ShapeDtypeStruct((m, GDN_WIDTH), BF16),
        scratch_shapes=[pltpu.VMEM((3, c + 8, width), F32), pltpu.VMEM((heads, HEAD_DIM, HEAD_DIM), F32)],
        compiler_params=_cparams(("parallel", "parallel", "arbitrary")),
        name="gdn",
    )(proj, proj, proj, proj, small, conv_w, conv_w, conv_w,
      _lane_row(a_log, SM_DECAY), _lane_row(dt_bias, SM_DECAY), o_gain.reshape(1, -1))


def _split_w_in(w_in, bn):
    sizes = (NSA_WIDTH, 6 * NSA_KV_WIDTH, 3 * NSA_HEADS, 3 * GDN_WIDTH, GDN_WIDTH, GDN_HEADS, GDN_HEADS, 2 * D_MODEL)
    offs = [int(o) for o in np.cumsum((0,) + sizes)]
    w16_t = w_in.T.astype(BF16)
    nq, nkv, ngate, gqkv, gz, gbeta, gdecay, merge = range(len(sizes))
    row_starts = [offs[part] + t for part in (merge, gqkv, nq, gz, nkv) for t in range(0, sizes[part], bn)]
    pad = jnp.zeros((LANES - (3 * NSA_HEADS + 2 * GDN_HEADS), w_in.shape[0]), BF16)
    small = jnp.concatenate([w16_t[offs[part]:offs[part + 1]] for part in (ngate, gbeta, gdecay)] + [pad], axis=0)
    return w16_t, row_starts, small


def _layer(x, p, g_mix, w_in, nsa_q_gain, nsa_kc_gain, nsa_ks_gain, nsa_kw_gain, cmp_pe_k, cmp_pe_v, cmp_wk1, cmp_wk2,
           cmp_wv1, cmp_wv2, gdn_conv_w, gdn_a_log, gdn_dt_bias, gdn_o_gain, w_up_nsa, w_up_gdn, w_out, g_mlp,
           w_mlp_in, w_mlp_out, g_ple, w_ple_gate, w_ple_proj, *, batch, seq):
    h = _rmsnorm(x, g_mix)
    w16_t, row_starts, w_small = _split_w_in(w_in, DENSE_TILES["in_proj"]["bn"])
    proj = _matmul_nt_rows(h, w16_t, row_starts, out_dtype=F32, name="in_proj", **DENSE_TILES["in_proj"])
    small = _matmul_nt(h, w_small, out_dtype=F32, name="in_proj_small", **DENSE_TILES["in_proj_small"])
    o_a = _nsa(proj, small, nsa_q_gain, nsa_kc_gain, nsa_ks_gain, nsa_kw_gain, cmp_pe_k, cmp_pe_v, cmp_wk1, cmp_wk2,
               cmp_wv1, cmp_wv2, batch=batch, seq=seq)
    o_b = _gdn(proj, small, gdn_conv_w, gdn_a_log, gdn_dt_bias, gdn_o_gain, batch=batch, seq=seq)
    mixed = _merge(o_a, w_up_nsa.astype(BF16), o_b, w_up_gdn.astype(BF16), proj, **DENSE_TILES["merge_up"])
    x, xg, ss = _out_proj(mixed, w_out.astype(BF16), x, g_mlp, **DENSE_TILES["out_proj"])
    hidden = _mlp_in(xg, w_mlp_in.astype(BF16), ss, **DENSE_TILES["mlp_in"])
    x, xg, ss = _mlp_out(hidden, w_mlp_out.astype(BF16), x, g_ple, **DENSE_TILES["mlp_out"])
    return _ple(xg, w_ple_gate.astype(BF16), p.astype(BF16), w_ple_proj.astype(BF16), x, ss, **DENSE_TILES["ple"])


def kernel(x, p, g_mix, w_in, nsa_q_gain, nsa_kc_gain, nsa_ks_gain, nsa_kw_gain, cmp_pe_k, cmp_pe_v, cmp_wk1, cmp_wk2,
           cmp_wv1, cmp_wv2, gdn_conv_w, gdn_a_log, gdn_dt_bias, gdn_o_gain, w_up_nsa, w_up_gdn, w_out, g_mlp,
           w_mlp_in, w_mlp_out, g_ple, w_ple_gate, w_ple_proj):
    batch, seq, d = x.shape
    depth = p.shape[0]
    assert seq // SEL_BLOCK <= LANES and seq % (4 * LANES) == 0
    y = x.reshape(batch * seq, d)
    for i in range(depth):
        y = _layer(y, p[i].reshape(batch * seq, -1), g_mix[i], w_in[i], nsa_q_gain[i], nsa_kc_gain[i], nsa_ks_gain[i],
                   nsa_kw_gain[i], cmp_pe_k[i], cmp_pe_v[i], cmp_wk1[i], cmp_wk2[i], cmp_wv1[i], cmp_wv2[i],
                   gdn_conv_w[i], gdn_a_log[i], gdn_dt_bias[i], gdn_o_gain[i], w_up_nsa[i], w_up_gdn[i], w_out[i],
                   g_mlp[i], w_mlp_in[i], w_mlp_out[i], g_ple[i], w_ple_gate[i], w_ple_proj[i], batch=batch, seq=seq)
    return y.reshape(batch, seq, d)
```

```python
import functools

import numpy as np
import jax
import jax.numpy as jnp
from jax import lax
from jax.experimental import pallas as pl
from jax.experimental.pallas import tpu as pltpu

D_MODEL = 4096
HEAD_DIM = 128
NSA_HEADS = 16
NSA_KV_HEADS = 4
NSA_GROUP = NSA_HEADS // NSA_KV_HEADS
CMP_LEN = 32
CMP_STRIDE = 16
CMP_HIDDEN = 4 * HEAD_DIM
SEL_BLOCK = 64
SEL_TOPK = 16
WINDOW = 512
GDN_HEADS = 16
GDN_CONV = 4
MLP_HIDDEN = 4 * D_MODEL
PLE_DIM = 256
ROPE_THETA = 10000.0
NORM_EPS = 1e-6
MASK_BIG = 1e30
BIAS_BIG = 2.0 ** 99
LOG2_E = 1.4426950408889634

NSA_WIDTH = NSA_HEADS * HEAD_DIM
NSA_KV_WIDTH = NSA_KV_HEADS * HEAD_DIM
GDN_WIDTH = GDN_HEADS * HEAD_DIM
LANES = 128
GDN_CHUNK = 128
GDN_HEAD_BLOCK = 16
INV_BASE = 16

OFF_MERGE = 0
OFF_GQKV = OFF_MERGE + 2 * D_MODEL
OFF_NQ = OFF_GQKV + 3 * GDN_WIDTH
OFF_GZ = OFF_NQ + NSA_WIDTH
OFF_NKV = OFF_GZ + GDN_WIDTH
BIG_WIDTH = OFF_NKV + 6 * NSA_KV_WIDTH
SM_GATE = 0
SM_BETA = 3 * NSA_HEADS
SM_DECAY = SM_BETA + GDN_HEADS

V7X_VMEM_LIMIT = 56 * 1024 * 1024

DENSE_TILES = {
    "in_proj": dict(bm=1024, bn=1024),
    "merge_up": dict(bm=1024, bn=512),
    "out_proj": dict(bm=1024, bn=512),
    "mlp_in": dict(bm=1024, bn=1024),
    "mlp_out": dict(bm=1024, bn=1024, bk=2048),
    "ple": dict(bm=1024, bn=512),
}

F32 = jnp.float32
BF16 = jnp.bfloat16


def _cparams(sem, vmem=V7X_VMEM_LIMIT):
    return pltpu.CompilerParams(dimension_semantics=sem, vmem_limit_bytes=vmem)


def _dot(a, b):
    return jnp.dot(a, b, preferred_element_type=F32)


def _dot_nt(a, b):
    return lax.dot_general(a, b, (((1,), (1,)), ((), ())), preferred_element_type=F32)


def _dot_tn(a, b):
    return lax.dot_general(a, b, (((0,), (0,)), ((), ())), preferred_element_type=F32)


def _split3(x):
    hi = x.astype(BF16)
    r = x - hi.astype(F32)
    mid = r.astype(BF16)
    lo = (r - mid.astype(F32)).astype(BF16)
    return hi, mid, lo


def _split2(x):
    hi = x.astype(BF16)
    lo = (x - hi.astype(F32)).astype(BF16)
    return hi, lo


def _dot_hp(a, b):
    return _dot(a[0], b[0]) + (_dot(a[0], b[1]) + _dot(a[1], b[0]))


def _sigmoid(x):
    return 0.5 * jnp.tanh(0.5 * x) + 0.5


def _silu(x):
    h = 0.5 * x
    return h * jnp.tanh(h) + h


def _iota(shape, dim):
    return lax.broadcasted_iota(jnp.int32, shape, dim)


def _rmsnorm_body(x_ref, g_ref, wt_ref, o_ref, sm_ref):
    x = x_ref[...]
    y = x * lax.rsqrt(jnp.mean(x * x, axis=-1, keepdims=True) + NORM_EPS)
    h = (y * g_ref[...]).astype(o_ref.dtype)
    o_ref[...] = h
    sm_ref[...] = _dot_nt(h, wt_ref[...])


def _rmsnorm(x, gain, w_small_t, *, rows=256):
    m, d = x.shape
    n_small = w_small_t.shape[0]
    return pl.pallas_call(
        _rmsnorm_body,
        grid=(m // rows,),
        in_specs=[pl.BlockSpec((rows, d), lambda i: (i, 0)), pl.BlockSpec((1, d), lambda i: (0, 0)),
                  pl.BlockSpec((n_small, d), lambda i: (0, 0))],
        out_specs=[pl.BlockSpec((rows, d), lambda i: (i, 0)), pl.BlockSpec((rows, n_small), lambda i: (i, 0))],
        out_shape=[jax.ShapeDtypeStruct((m, d), BF16), jax.ShapeDtypeStruct((m, n_small), F32)],
        compiler_params=_cparams(("parallel",)),
        name="rmsnorm",
    )(x, gain.reshape(1, d), w_small_t)


def _row_spec(bm, width):
    return pl.BlockSpec((bm, width), lambda i, j: (i, 0))


def _mm_nt_rows_body(rows_ref, a_ref, bt_ref, o_ref):
    del rows_ref
    o_ref[...] = _dot_nt(a_ref[...], bt_ref[...]).astype(o_ref.dtype)


def _matmul_nt_rows(a, b_t, row_starts, *, bm, bn, out_dtype, name):
    m, kdim = a.shape
    n_col = len(row_starts)
    align = 16
    assert all(r % align == 0 and r + bn <= b_t.shape[0] for r in row_starts)
    grid_spec = pltpu.PrefetchScalarGridSpec(
        num_scalar_prefetch=1,
        grid=(m // bm, n_col),
        in_specs=[pl.BlockSpec((bm, kdim), lambda i, j, rows: (i, 0)),
                  pl.BlockSpec((pl.Element(bn), pl.Element(kdim)), lambda i, j, rows: (rows[j] * align, 0))],
        out_specs=pl.BlockSpec((bm, bn), lambda i, j, rows: (i, j)),
    )
    return pl.pallas_call(
        _mm_nt_rows_body,
        grid_spec=grid_spec,
        out_shape=jax.ShapeDtypeStruct((m, n_col * bn), out_dtype),
        compiler_params=_cparams(("parallel", "parallel")),
        name=name,
    )(jnp.asarray([r // align for r in row_starts], jnp.int32), a, b_t)


def _norm_outputs(x, g_ref, xg_ref, ss_ref):
    xg_ref[...] = (x * g_ref[...]).astype(BF16)
    sq = x * x
    ss_ref[...] = functools.reduce(jnp.add, [sq[:, c * LANES:(c + 1) * LANES] for c in range(sq.shape[1] // LANES)])


def _row_scale(ss_ref, width):
    return lax.rsqrt(jnp.sum(ss_ref[...], axis=-1, keepdims=True) * (1.0 / width) + NORM_EPS)


def _norm_out_shapes(m, n, bn):
    return [jax.ShapeDtypeStruct((m, n), F32), jax.ShapeDtypeStruct((m, n), BF16),
            jax.ShapeDtypeStruct((m, n // bn * LANES), F32)]


def _out_proj_body(a_ref, b_ref, x_ref, g_ref, o_ref, xg_ref, ss_ref):
    x = x_ref[...] + _dot(a_ref[...], b_ref[...])
    o_ref[...] = x
    _norm_outputs(x, g_ref, xg_ref, ss_ref)


def _out_proj(a, b, x, gain, *, bm, bn):
    m, kdim = a.shape
    n = b.shape[1]
    tile = pl.BlockSpec((bm, bn), lambda i, j: (i, j))
    return pl.pallas_call(
        _out_proj_body,
        grid=(m // bm, n // bn),
        in_specs=[_row_spec(bm, kdim), pl.BlockSpec((kdim, bn), lambda i, j: (0, j)),
                  tile, pl.BlockSpec((1, bn), lambda i, j: (0, j))],
        out_specs=[tile, tile, pl.BlockSpec((bm, LANES), lambda i, j: (i, j))],
        out_shape=_norm_out_shapes(m, n, bn),
        compiler_params=_cparams(("parallel", "parallel")),
        name="out_proj",
    )(a, b, x, gain.reshape(1, n))


def _mlp_in_body(a_ref, b_ref, ss_ref, o_ref, *, width):
    r = jnp.maximum(_dot(a_ref[...], b_ref[...]) * _row_scale(ss_ref, width), 0.0)
    o_ref[...] = (r * r).astype(o_ref.dtype)


def _mlp_in(xg, b, ss, *, bm, bn):
    m, kdim = xg.shape
    n = b.shape[1]
    return pl.pallas_call(
        functools.partial(_mlp_in_body, width=kdim),
        grid=(m // bm, n // bn),
        in_specs=[pl.BlockSpec((bm, kdim), lambda i, j: (i, 0)), pl.BlockSpec((kdim, bn), lambda i, j: (0, j)),
                  pl.BlockSpec((bm, ss.shape[1]), lambda i, j: (i, 0))],
        out_specs=pl.BlockSpec((bm, bn), lambda i, j: (i, j)),
        out_shape=jax.ShapeDtypeStruct((m, n), BF16),
        compiler_params=_cparams(("parallel", "parallel")),
        name="mlp_in",
    )(xg, b, ss)


def _mlp_out_body(a_ref, b_ref, res_ref, g_ref, o_ref, xg_ref, ss_ref, *, nk):
    k = pl.program_id(2)

    @pl.when(k == 0)
    def _():
        o_ref[...] = res_ref[...]

    o_ref[...] += _dot(a_ref[...], b_ref[...])

    @pl.when(k == nk - 1)
    def _():
        _norm_outputs(o_ref[...], g_ref, xg_ref, ss_ref)


def _mlp_out(a, b, res, gain, *, bm, bn, bk):
    m, kdim = a.shape
    n = b.shape[1]
    nk = kdim // bk
    tile = pl.BlockSpec((bm, bn), lambda i, j, k: (i, j))
    return pl.pallas_call(
        functools.partial(_mlp_out_body, nk=nk),
        grid=(m // bm, n // bn, nk),
        in_specs=[pl.BlockSpec((bm, bk), lambda i, j, k: (i, k)), pl.BlockSpec((bk, bn), lambda i, j, k: (k, j)),
                  tile, pl.BlockSpec((1, bn), lambda i, j, k: (0, j))],
        out_specs=[tile, tile, pl.BlockSpec((bm, LANES), lambda i, j, k: (i, j))],
        out_shape=_norm_out_shapes(m, n, bn),
        compiler_params=_cparams(("parallel", "parallel", "arbitrary")),
        name="mlp_out",
    )(a, b, res, gain.reshape(1, n))


def _merge_body(oa_ref, wa_ref, ob_ref, wb_ref, ga_ref, gb_ref, o_ref):
    ya = _dot(oa_ref[...], wa_ref[...])
    yb = _dot(ob_ref[...], wb_ref[...])
    o_ref[...] = (_sigmoid(ga_ref[...]) * ya + _sigmoid(gb_ref[...]) * yb).astype(o_ref.dtype)


def _merge(o_a, w_a, o_b, w_b, proj, *, bm, bn):
    m, ka = o_a.shape
    kb = o_b.shape[1]
    n = w_a.shape[1]
    ga_blk = OFF_MERGE // bn
    gb_blk = (OFF_MERGE + D_MODEL) // bn
    return pl.pallas_call(
        _merge_body,
        grid=(m // bm, n // bn),
        in_specs=[
            _row_spec(bm, ka),
            pl.BlockSpec((ka, bn), lambda i, j: (0, j)),
            _row_spec(bm, kb),
            pl.BlockSpec((kb, bn), lambda i, j: (0, j)),
            pl.BlockSpec((bm, bn), lambda i, j: (i, ga_blk + j)),
            pl.BlockSpec((bm, bn), lambda i, j: (i, gb_blk + j)),
        ],
        out_specs=pl.BlockSpec((bm, bn), lambda i, j: (i, j)),
        out_shape=jax.ShapeDtypeStruct((m, n), BF16),
        compiler_params=_cparams(("parallel", "parallel")),
        name="merge_up",
    )(o_a, w_a, o_b, w_b, proj, proj)


def _ple_body(xg_ref, wg_ref, p_ref, wp_ref, x_ref, ss_ref, o_ref, *, width):
    gate = _sigmoid(_dot(xg_ref[...], wg_ref[...]) * _row_scale(ss_ref, width))
    o_ref[...] = x_ref[...] + gate * _dot(p_ref[...], wp_ref[...])


def _ple(xg, w_gate, p, w_proj, x, ss, *, bm, bn):
    m, kd = xg.shape
    kp = p.shape[1]
    n = w_gate.shape[1]
    return pl.pallas_call(
        functools.partial(_ple_body, width=kd),
        grid=(m // bm, n // bn),
        in_specs=[
            _row_spec(bm, kd),
            pl.BlockSpec((kd, bn), lambda i, j: (0, j)),
            _row_spec(bm, kp),
            pl.BlockSpec((kp, bn), lambda i, j: (0, j)),
            pl.BlockSpec((bm, bn), lambda i, j: (i, j)),
            _row_spec(bm, ss.shape[1]),
        ],
        out_specs=pl.BlockSpec((bm, bn), lambda i, j: (i, j)),
        out_shape=jax.ShapeDtypeStruct((m, n), F32),
        compiler_params=_cparams(("parallel", "parallel")),
        name="ple",
    )(xg, w_gate, p, w_proj, x, ss)


def _norm_rope(x, gain, cos, sin_signed):
    y = x * lax.rsqrt(jnp.mean(x * x, axis=-1, keepdims=True) + NORM_EPS) * gain
    return y * cos + pltpu.roll(y, HEAD_DIM // 2, 1) * sin_signed


def _nsa_prep_body(q_ref, ks_ref, vs_ref, kw_ref, vw_ref, cos_ref, sin_ref, qg_ref, ksg_ref, kwg_ref,
                   qo_ref, ksat_ref, vso_ref, kwt_ref, vwo_ref, *, rows, blocks_per_seq):
    cos = cos_ref[...]
    sin = sin_ref[...]
    q_scale = (HEAD_DIM ** -0.5) * LOG2_E
    lanes = [slice(h * HEAD_DIM, (h + 1) * HEAD_DIM) for h in range(NSA_HEADS)]
    kv_lanes = lanes[:NSA_KV_HEADS]
    heads = ([(q_ref[:, sl], qg_ref) for sl in lanes] + [(ks_ref[:, sl], ksg_ref) for sl in kv_lanes]
             + [(kw_ref[:, sl], kwg_ref) for sl in kv_lanes])
    inv = [lax.rsqrt(jnp.mean(x * x, axis=-1, keepdims=True) + NORM_EPS) for x, _ in heads]
    normed = [x * r * g_ref[...] for (x, g_ref), r in zip(heads, inv)]
    turned = [pltpu.roll(y, HEAD_DIM // 2, 1) for y in normed]
    roped = [y * cos + z * sin for y, z in zip(normed, turned)]
    for sl, y in zip(lanes, roped[:NSA_HEADS]):
        qo_ref[:, sl] = (y * q_scale).astype(BF16)
    t = (pl.program_id(0) % blocks_per_seq) * rows + _iota((LANES, rows), 1)
    onehot_t = (_iota((LANES, rows), 0) == t // SEL_BLOCK).astype(BF16)
    for g in range(NSA_KV_HEADS):
        ksat_ref[g, 0:HEAD_DIM, :] = roped[NSA_HEADS + g].T.astype(BF16)
        ksat_ref[g, HEAD_DIM:2 * HEAD_DIM, :] = onehot_t
        kwt_ref[g] = roped[NSA_HEADS + NSA_KV_HEADS + g].T.astype(BF16)
    vso_ref[...] = vs_ref[...].astype(BF16)
    vwo_ref[...] = vw_ref[...].astype(BF16)


def _nsa_prep(proj, cos, sin_signed, q_gain, ks_gain, kw_gain, *, batch, seq, rows=256):
    m = proj.shape[0]
    bps = seq // rows
    kvw = NSA_KV_WIDTH
    kv_blk = OFF_NKV // kvw

    def col(c):
        return lambda i: (i, c)

    tab = pl.BlockSpec((rows, HEAD_DIM), lambda i: (i % bps, 0))
    gain = pl.BlockSpec((1, HEAD_DIM), lambda i: (0, 0))
    return pl.pallas_call(
        functools.partial(_nsa_prep_body, rows=rows, blocks_per_seq=bps),
        grid=(m // rows,),
        in_specs=[
            pl.BlockSpec((rows, NSA_WIDTH), col(OFF_NQ // NSA_WIDTH)),
            pl.BlockSpec((rows, kvw), col(kv_blk + 2)),
            pl.BlockSpec((rows, kvw), col(kv_blk + 3)),
            pl.BlockSpec((rows, kvw), col(kv_blk + 4)),
            pl.BlockSpec((rows, kvw), col(kv_blk + 5)),
            tab, tab, gain, gain, gain,
        ],
        out_specs=[
            pl.BlockSpec((rows, NSA_WIDTH), col(0)),
            pl.BlockSpec((None, NSA_KV_HEADS, 2 * HEAD_DIM, rows), lambda i: (i // bps, 0, 0, i % bps)),
            pl.BlockSpec((rows, kvw), col(0)),
            pl.BlockSpec((None, NSA_KV_HEADS, HEAD_DIM, rows), lambda i: (i // bps, 0, 0, i % bps)),
            pl.BlockSpec((rows, kvw), col(0)),
        ],
        out_shape=[
            jax.ShapeDtypeStruct((m, NSA_WIDTH), BF16),
            jax.ShapeDtypeStruct((batch, NSA_KV_HEADS, 2 * HEAD_DIM, seq), BF16),
            jax.ShapeDtypeStruct((m, kvw), BF16),
            jax.ShapeDtypeStruct((batch, NSA_KV_HEADS, HEAD_DIM, seq), BF16),
            jax.ShapeDtypeStruct((m, kvw), BF16),
        ],
        compiler_params=_cparams(("parallel",)),
        name="nsa_prep",
    )(proj, proj, proj, proj, proj, cos, sin_signed, q_gain.reshape(1, -1), ks_gain.reshape(1, -1), kw_gain.reshape(1, -1))


def _gelu_tanh(x):
    return 0.5 * x * (1.0 + jnp.tanh(np.sqrt(2.0 / np.pi).astype(np.float32) * (x + 0.044715 * (x * x * x))))


def _compress_one(x_ref, pe_ref, w1_ref, w2_ref, n_chunks):
    half = CMP_LEN // 2
    first = jnp.zeros((n_chunks, CMP_HIDDEN), F32)
    second = jnp.zeros((n_chunks, CMP_HIDDEN), F32)
    for l in range(half):
        xl = x_ref[pl.ds(l, n_chunks, stride=CMP_STRIDE), :]
        first += _dot((xl + pe_ref[l:l + 1, :]).astype(BF16), w1_ref[l])
        second += _dot((xl + pe_ref[half + l:half + l + 1, :]).astype(BF16), w1_ref[half + l])
    hid = _gelu_tanh(first + pltpu.roll(second, n_chunks - 1, 0))
    return _dot(hid.astype(BF16), w2_ref[...])


def _compress_body(xk_ref, xv_ref, pek_ref, pev_ref, wk1_ref, wk2_ref, wv1_ref, wv2_ref, kg_ref, cos_ref, sin_ref,
                   kto_ref, vo_ref, *, n_chunks):
    k = _compress_one(xk_ref, pek_ref, wk1_ref, wk2_ref, n_chunks)
    kto_ref[...] = _norm_rope(k, kg_ref[...], cos_ref[...], sin_ref[...]).T.astype(BF16)
    vo_ref[...] = _compress_one(xv_ref, pev_ref, wv1_ref, wv2_ref, n_chunks).astype(BF16)


def _compress(proj, pe_k, pe_v, wk1, wk2, wv1, wv2, kc_gain, cos_c, sin_c, *, batch, seq):
    n_chunks = seq // CMP_STRIDE
    kc_blk = OFF_NKV // HEAD_DIM
    vc_blk = (OFF_NKV + NSA_KV_WIDTH) // HEAD_DIM

    def full(shape):
        return pl.BlockSpec(shape, lambda b, g: (0,) * len(shape))

    return pl.pallas_call(
        functools.partial(_compress_body, n_chunks=n_chunks),
        grid=(batch, NSA_KV_HEADS),
        in_specs=[
            pl.BlockSpec((seq, HEAD_DIM), lambda b, g: (b, kc_blk + g)),
            pl.BlockSpec((seq, HEAD_DIM), lambda b, g: (b, vc_blk + g)),
            full((CMP_LEN, HEAD_DIM)), full((CMP_LEN, HEAD_DIM)),
            full((CMP_LEN, HEAD_DIM, CMP_HIDDEN)), full((CMP_HIDDEN, HEAD_DIM)),
            full((CMP_LEN, HEAD_DIM, CMP_HIDDEN)), full((CMP_HIDDEN, HEAD_DIM)),
            full((1, HEAD_DIM)), full((n_chunks, HEAD_DIM)), full((n_chunks, HEAD_DIM)),
        ],
        out_specs=[
            pl.BlockSpec((None, None, HEAD_DIM, n_chunks), lambda b, g: (b, g, 0, 0)),
            pl.BlockSpec((None, None, n_chunks, HEAD_DIM), lambda b, g: (b, g, 0, 0)),
        ],
        out_shape=[
            jax.ShapeDtypeStruct((batch, NSA_KV_HEADS, HEAD_DIM, n_chunks), BF16),
            jax.ShapeDtypeStruct((batch, NSA_KV_HEADS, n_chunks, HEAD_DIM), BF16),
        ],
        compiler_params=_cparams(("parallel", "parallel")),
        name="nsa_compress",
    )(proj, proj, pe_k, pe_v, wk1, wk2, wv1, wv2, kc_gain.reshape(1, -1), cos_c, sin_c)


def _softmax2(s, mask):
    s = jnp.where(mask, s, -MASK_BIG)
    e = jnp.where(mask, jnp.exp2(s - jnp.max(s, axis=-1, keepdims=True)), 0.0)
    return e * (1.0 / jnp.maximum(jnp.sum(e, axis=-1, keepdims=True), 1e-30))


def _select_blocks(imp, t0, top_k):
    tq = imp.shape[0]
    imp_t = imp.T
    n_lane = imp_t.shape[0]
    j = _iota((n_lane, tq), 0)
    cur = (t0 + _iota((n_lane, tq), 1)) // SEL_BLOCK
    valid = j <= cur
    forced = (j == 0) | (j == cur) | (j == cur - 1)
    score = jnp.where(valid & jnp.logical_not(forced), imp_t, -jnp.inf)
    for _ in range(top_k - 3):
        best = jnp.max(score, axis=0, keepdims=True)
        first = jnp.min(jnp.where(score == best, j, n_lane), axis=0, keepdims=True)
        score = jnp.where(j == first, -jnp.inf, score)
    sel = jnp.where(valid & (score == -jnp.inf), 0.0, -BIAS_BIG)
    return sel.T.astype(BF16)


def _lane_column(x, lane):
    return jnp.sum(jnp.where(_iota(x.shape, 1) == lane, x, 0.0), axis=1, keepdims=True)


def _flash_scores(s, m_ref, l_ref, r):
    chunks = [s[:, c * LANES:(c + 1) * LANES] for c in range(s.shape[1] // LANES)]
    m_prev = m_ref[r]
    row_max = jnp.max(functools.reduce(jnp.maximum, chunks), axis=-1, keepdims=True)
    m_new = jnp.maximum(m_prev, jnp.broadcast_to(row_max, m_prev.shape))
    alpha = jnp.exp2(m_prev - m_new)
    ps = [jnp.exp2(ch - m_new) for ch in chunks]
    l_ref[r] = alpha * l_ref[r] + functools.reduce(jnp.add, ps)
    m_ref[r] = m_new
    return jnp.concatenate([x.astype(BF16) for x in ps], axis=1), alpha


def _attend_body(q_ref, kct_ref, vc_ref, ov_ref, ksat_ref, vs_ref, kwt_ref, vw_ref, sm_ref, o_ref,
                 m_scr, l_scr, acc_scr, oc_scr, *, tq, tk, tail, top_k):
    g = pl.program_id(1)
    t0 = pl.program_id(2) * tq
    tpos = t0 + _iota((tq, 1), 0)
    heads = range(NSA_GROUP)
    lanes = [slice(r * HEAD_DIM, (r + 1) * HEAD_DIM) for r in heads]
    q = [q_ref[:, sl] for sl in lanes]

    n_cmp = kct_ref.shape[1]
    visible = _iota((1, n_cmp), 1) * CMP_STRIDE + (CMP_LEN - 1) <= tpos
    kct = kct_ref[...]
    vc = vc_ref[...]
    probs = [_softmax2(s, visible) for s in [_dot(x, kct) for x in q]]
    for r, p in zip(heads, probs):
        oc_scr[r] = _dot(p.astype(BF16), vc)
    ov = ov_ref[...]
    p_hi, p_mid, p_lo = _split3(functools.reduce(jnp.add, probs))
    imp = _dot(p_hi, ov) + (_dot(p_mid, ov) + _dot(p_lo, ov))
    sb = _select_blocks(imp, t0, top_k)

    qa = [jnp.concatenate([x, sb], axis=1) for x in q]
    m_scr[...] = jnp.full(m_scr.shape, -MASK_BIG, F32)
    l_scr[...] = jnp.zeros(l_scr.shape, F32)
    acc_scr[...] = jnp.zeros(acc_scr.shape, F32)

    def tile(k0, width, causal):
        k_tile = ksat_ref[:, pl.ds(k0, width)]
        v_tile = vs_ref[pl.ds(k0, width), :]
        scores = [_dot(x, k_tile) for x in qa]
        if causal:
            keep = k0 + _iota((1, width), 1) <= tpos
            scores = [jnp.where(keep, s, -MASK_BIG) for s in scores]
        stats = [_flash_scores(scores[r], m_scr, l_scr, r) for r in heads]
        for r, (p, alpha) in zip(heads, stats):
            acc_scr[r] = alpha * acc_scr[r] + _dot(p, v_tile)

    n_full = t0 // tk

    def full_body(kt, carry):
        tile(pl.multiple_of(kt * tk, tk), tk, False)
        return carry

    lax.fori_loop(0, n_full, full_body, 0)
    done = n_full * tk

    def tail_body(j, carry):
        tile(pl.multiple_of(done + j * tail, tail), tail, True)
        return carry

    lax.fori_loop(0, (t0 + tq - done + tail - 1) // tail, tail_body, 0)

    band = WINDOW + tq
    s0 = pl.multiple_of(jnp.maximum(t0 - WINDOW, 0), tq)
    kw_band = kwt_ref[:, pl.ds(s0, band)]
    vw_band = vw_ref[pl.ds(s0, band), :]
    dist = tpos - (s0 + _iota((1, band), 1))
    in_window = (dist >= 0) & (dist < WINDOW)
    scores = [jnp.where(in_window, _dot(x, kw_band), -MASK_BIG) for x in q]
    weights = [jnp.exp2(s - jnp.max(s, axis=-1, keepdims=True)) for s in scores]
    inv = [1.0 / jnp.maximum(jnp.sum(e, axis=-1, keepdims=True), 1e-30) for e in weights]
    o_w = [_dot(e.astype(BF16), vw_band) * r for e, r in zip(weights, inv)]

    gates = _sigmoid(sm_ref[...])
    for r, sl in zip(heads, lanes):
        o_s = acc_scr[r] * (1.0 / jnp.maximum(jnp.sum(l_scr[r], axis=-1, keepdims=True), 1e-30))
        lane = SM_GATE + (g * NSA_GROUP + r) * 3
        out = (_lane_column(gates, lane) * oc_scr[r] + _lane_column(gates, lane + 1) * o_s
               + _lane_column(gates, lane + 2) * o_w[r])
        o_ref[:, sl] = out.astype(BF16)


def _attend(qn, k_cmp_t, v_cmp, overlap, ksa_t, vs, kw_t, vw, small, *, batch, seq, tq=256, tk=1024, tail=512):
    m = qn.shape[0]
    gw = NSA_GROUP * HEAD_DIM
    nq = seq // tq
    tk = min(tk, seq)
    n_cmp = v_cmp.shape[2]
    top_k = min(SEL_TOPK, seq // SEL_BLOCK)

    def tile_spec(width):
        return pl.BlockSpec((tq, width), lambda b, g, i: (b * nq + i, g))

    def seq_spec(width):
        return pl.BlockSpec((seq, width), lambda b, g, i: (b, g))

    def group_spec(rows, cols):
        return pl.BlockSpec((None, None, rows, cols), lambda b, g, i: (b, g, 0, 0))

    stat = pltpu.VMEM((NSA_GROUP, tq, LANES), F32)
    return pl.pallas_call(
        functools.partial(_attend_body, tq=tq, tk=tk, tail=tail, top_k=top_k),
        grid=(batch, NSA_KV_HEADS, nq),
        in_specs=[
            tile_spec(gw),
            group_spec(HEAD_DIM, n_cmp), group_spec(n_cmp, HEAD_DIM),
            pl.BlockSpec((n_cmp, LANES), lambda b, g, i: (0, 0)),
            group_spec(2 * HEAD_DIM, seq), seq_spec(HEAD_DIM), group_spec(HEAD_DIM, seq), seq_spec(HEAD_DIM),
            pl.BlockSpec((tq, LANES), lambda b, g, i: (b * nq + i, 0)),
        ],
        out_specs=tile_spec(gw),
        out_shape=jax.ShapeDtypeStruct((m, NSA_WIDTH), BF16),
        scratch_shapes=[stat, stat, stat, stat],
        compiler_params=_cparams(("parallel", "parallel", "arbitrary")),
        name="nsa_attend",
    )(qn, k_cmp_t, v_cmp, overlap, ksa_t, vs, kw_t, vw, small)


def _rope_tables(pos):
    half = HEAD_DIM // 2
    inv_freq = ROPE_THETA ** (-jnp.arange(half, dtype=F32) / half)
    ang = pos[:, None] * inv_freq[None, :]
    cos, sin = jnp.cos(ang), jnp.sin(ang)
    return jnp.concatenate([cos, cos], axis=-1), jnp.concatenate([-sin, sin], axis=-1)


def _overlap_matrix(n_cmp_pad, n_cmp):
    c_start = np.arange(n_cmp_pad) * CMP_STRIDE
    s_start = np.arange(LANES) * SEL_BLOCK
    ov = (c_start[:, None] < s_start[None, :] + SEL_BLOCK) & (c_start[:, None] + CMP_LEN > s_start[None, :])
    ov &= (np.arange(n_cmp_pad) < n_cmp)[:, None]
    return jnp.asarray(ov.astype(np.float32), dtype=BF16)


def _nsa(proj, small, q_gain, kc_gain, ks_gain, kw_gain, pe_k, pe_v, wk1, wk2, wv1, wv2, *, batch, seq):
    n_chunks = seq // CMP_STRIDE
    cos, sin = _rope_tables(jnp.arange(seq, dtype=F32))
    cmp_end = jnp.arange(n_chunks) * CMP_STRIDE + (CMP_LEN - 1)
    cos_c, sin_c = _rope_tables(cmp_end.astype(F32))
    qn, ksa_t, vs, kw_t, vw = _nsa_prep(proj, cos, sin, q_gain, ks_gain, kw_gain, batch=batch, seq=seq)
    k_cmp_t, v_cmp = _compress(proj, pe_k, pe_v, wk1.astype(BF16), wk2.astype(BF16), wv1.astype(BF16),
                               wv2.astype(BF16), kc_gain, cos_c, sin_c, batch=batch, seq=seq)
    overlap = _overlap_matrix(n_chunks, n_chunks - 1)
    return _attend(qn, k_cmp_t, v_cmp, overlap, ksa_t, vs, kw_t, vw, small, batch=batch, seq=seq)


def _unit_lower_inverses(mats):
    c = mats[0].shape[0]
    row = _iota((c, c), 0)
    col = _iota((c, c), 1)
    eye = (row == col).astype(F32)
    diag = row // INV_BASE == col // INV_BASE
    xs = [jnp.where(diag, -a, 0.0) for a in mats]
    ts = [eye + x for x in xs]
    x16 = [x.astype(BF16) for x in xs]
    powers = [_dot(x, x) for x in x16]
    span = 2
    while span < INV_BASE:
        p16 = [p.astype(BF16) for p in powers]
        if 2 * span < INV_BASE:
            both = [_dot(p, jnp.concatenate([p, t.astype(BF16)], axis=1)) for p, t in zip(p16, ts)]
            ts = [t + b[:, c:] for t, b in zip(ts, both)]
            powers = [b[:, :c] for b in both]
        else:
            ts = [t + _dot(p, t.astype(BF16)) for p, t in zip(p16, ts)]
        span *= 2
    size = INV_BASE
    while size < c:
        off = (row // (2 * size) == col // (2 * size)) & (row // size != col // size)
        t16 = [t.astype(BF16) for t in ts]
        inner = [_dot(jnp.where(off, a, 0.0).astype(BF16), t) for a, t in zip(mats, t16)]
        ts = [t - _dot(t2, i.astype(BF16)) for t, t2, i in zip(ts, t16, inner)]
        size *= 2
    resid = [eye - _dot_hp(_split2(eye + a), _split2(t)) for a, t in zip(mats, ts)]
    return [t + _dot(t.astype(BF16), r.astype(BF16)) for t, r in zip(ts, resid)]


def _gdn_body(xq_ref, xk_ref, xv_ref, z_ref, sm_ref, wq_ref, wk_ref, wv_ref, alog_ref, dtb_ref, og_ref, o_ref,
              xs_scr, state_scr, *, heads):
    c = GDN_CHUNK
    hb = pl.program_id(1)
    step = pl.program_id(2)
    halo = 8
    hs = range(heads)

    @pl.when(step == 0)
    def _():
        xs_scr[:, 0:halo, :] = jnp.zeros((3, halo, xs_scr.shape[2]), F32)
        state_scr[...] = jnp.zeros_like(state_scr)

    conv = []
    for idx, (x_ref, w_ref) in enumerate(((xq_ref, wq_ref), (xk_ref, wk_ref), (xv_ref, wv_ref))):
        xs_scr[idx, halo:halo + c, :] = x_ref[...]
        y = jnp.zeros(x_ref.shape, F32)
        for tap in range(GDN_CONV):
            y = y + w_ref[tap:tap + 1, :] * xs_scr[idx, pl.ds(halo - (GDN_CONV - 1) + tap, c), :]
        xs_scr[idx, 0:halo, :] = xs_scr[idx, c:c + halo, :]
        conv.append(_silu(y))
    qc, kc, vc = conv

    sm = sm_ref[...]
    beta_all = _sigmoid(sm)
    x = sm + dtb_ref[...]
    softplus = jnp.maximum(x, 0.0) + jnp.log1p(jnp.exp(-jnp.abs(x)))
    g_all = -(jnp.exp(alog_ref[...]) * softplus)
    row = _iota((c, c), 0)
    col = _iota((c, c), 1)
    tril = row >= col
    ones_tril = tril.astype(BF16)
    g_hi, g_mid, g_lo = _split3(g_all)
    gc_all = _dot(ones_tril, g_hi) + (_dot(ones_tril, g_mid) + _dot(ones_tril, g_lo))
    gc_all_t = gc_all.T

    def head_lanes(h):
        return slice(h * HEAD_DIM, (h + 1) * HEAD_DIM)

    qs, ks, gammas, g_cols, g_lasts, betas = [], [], [], [], [], []
    for h in hs:
        q = qc[:, head_lanes(h)]
        k = kc[:, head_lanes(h)]
        qs.append(q * lax.rsqrt(jnp.sum(q * q, axis=-1, keepdims=True) + NORM_EPS) * (HEAD_DIM ** -0.5))
        ks.append(k * lax.rsqrt(jnp.sum(k * k, axis=-1, keepdims=True) + NORM_EPS))
        head = hb * heads + h
        betas.append(_lane_column(beta_all, SM_BETA + head))
        g_col = _lane_column(gc_all, SM_DECAY + head)
        g_row = jnp.sum(jnp.where(_iota((LANES, c), 0) == SM_DECAY + head, gc_all_t, 0.0), axis=0, keepdims=True)
        gammas.append(jnp.exp(jnp.where(tril, g_col - g_row, -MASK_BIG)))
        g_cols.append(g_col)
        g_lasts.append(g_row[:, c - 1:c])
    decays = [jnp.exp(g) for g in g_cols]
    kbs = [k * b for k, b in zip(ks, betas)]
    k16 = [k.astype(BF16) for k in ks]
    kk_qk = [_dot_nt(jnp.concatenate([kb, q], axis=0).astype(BF16), kk) for kb, q, kk in zip(kbs, qs, k16)]
    t_inv = _unit_lower_inverses([jnp.where(row > col, x[:c] * gm, 0.0) for x, gm in zip(kk_qk, gammas)])
    uw = [_dot(t.astype(BF16), jnp.concatenate([vc[:, head_lanes(h)] * b, kb * d], axis=1).astype(BF16))
          for h, t, b, kb, d in zip(hs, t_inv, betas, kbs, decays)]
    states = [state_scr[h] for h in hs]
    ws_qs = [_dot(jnp.concatenate([u[:, HEAD_DIM:], q * d], axis=0).astype(BF16), s.astype(BF16))
             for u, q, d, s in zip(uw, qs, decays, states)]
    v_new = [(u[:, :HEAD_DIM] - w[:c]).astype(BF16) for u, w in zip(uw, ws_qs)]
    outs = [w[c:] + _dot((x[c:] * gm).astype(BF16), v) for w, x, gm, v in zip(ws_qs, kk_qk, gammas, v_new)]
    for h in hs:
        k_tail = (ks[h] * jnp.exp(g_lasts[h] - g_cols[h])).astype(BF16)
        state_scr[h] = states[h] * jnp.exp(g_lasts[h]) + _dot_tn(k_tail, v_new[h])
    for h in hs:
        o = outs[h]
        o = o * lax.rsqrt(jnp.mean(o * o, axis=-1, keepdims=True) + NORM_EPS) * og_ref[...]
        o_ref[:, head_lanes(h)] = (o * _silu(z_ref[:, head_lanes(h)])).astype(BF16)


def _lane_row(values, offset):
    return jnp.zeros((1, LANES), F32).at[0, offset:offset + values.shape[0]].set(values)


def _gdn(proj, small, conv_w, a_log, dt_bias, o_gain, *, batch, seq):
    m = proj.shape[0]
    c = GDN_CHUNK
    heads = GDN_HEAD_BLOCK
    width = heads * HEAD_DIM
    n_steps = seq // c

    def act_spec(off):
        base = off // width
        return pl.BlockSpec((c, width), lambda b, hb, s: (b * n_steps + s, base + hb))

    def conv_spec(part):
        base = part * GDN_WIDTH // width
        return pl.BlockSpec((GDN_CONV, width), lambda b, hb, s: (0, base + hb))

    row_spec = pl.BlockSpec((1, LANES), lambda b, hb, s: (0, 0))
    return pl.pallas_call(
        functools.partial(_gdn_body, heads=heads),
        grid=(batch, GDN_HEADS // heads, n_steps),
        in_specs=[
            act_spec(OFF_GQKV), act_spec(OFF_GQKV + GDN_WIDTH), act_spec(OFF_GQKV + 2 * GDN_WIDTH), act_spec(OFF_GZ),
            pl.BlockSpec((c, LANES), lambda b, hb, s: (b * n_steps + s, 0)),
            conv_spec(0), conv_spec(1), conv_spec(2),
            row_spec, row_spec, row_spec,
        ],
        out_specs=pl.BlockSpec((c, width), lambda b, hb, s: (b * n_steps + s, hb)),
        out_shape=jax.ShapeDtypeStruct((m, GDN_WIDTH), BF16),
        scratch_shapes=[pltpu.VMEM((3, c + 8, width), F32), pltpu.VMEM((heads, HEAD_DIM, HEAD_DIM), F32)],
        compiler_params=_cparams(("parallel", "parallel", "arbitrary")),
        name="gdn",
    )(proj, proj, proj, proj, small, conv_w, conv_w, conv_w,
      _lane_row(a_log, SM_DECAY), _lane_row(dt_bias, SM_DECAY), o_gain.reshape(1, -1))


def _split_w_in(w_in, bn):
    sizes = (NSA_WIDTH, 6 * NSA_KV_WIDTH, 3 * NSA_HEADS, 3 * GDN_WIDTH, GDN_WIDTH, GDN_HEADS, GDN_HEADS, 2 * D_MODEL)
    offs = [int(o) for o in np.cumsum((0,) + sizes)]
    w16_t = w_in.T.astype(BF16)
    nq, nkv, ngate, gqkv, gz, gbeta, gdecay, merge = range(len(sizes))
    row_starts = [offs[part] + t for part in (merge, gqkv, nq, gz, nkv) for t in range(0, sizes[part], bn)]
    pad = jnp.zeros((LANES - (3 * NSA_HEADS + 2 * GDN_HEADS), w_in.shape[0]), BF16)
    small = jnp.concatenate([w16_t[offs[part]:offs[part + 1]] for part in (ngate, gbeta, gdecay)] + [pad], axis=0)
    return w16_t, row_starts, small


def _layer(x, p, g_mix, w_in, nsa_q_gain, nsa_kc_gain, nsa_ks_gain, nsa_kw_gain, cmp_pe_k, cmp_pe_v, cmp_wk1, cmp_wk2,
           cmp_wv1, cmp_wv2, gdn_conv_w, gdn_a_log, gdn_dt_bias, gdn_o_gain, w_up_nsa, w_up_gdn, w_out, g_mlp,
           w_mlp_in, w_mlp_out, g_ple, w_ple_gate, w_ple_proj, *, batch, seq):
    w16_t, row_starts, w_small = _split_w_in(w_in, DENSE_TILES["in_proj"]["bn"])
    h, small = _rmsnorm(x, g_mix, w_small)
    proj = _matmul_nt_rows(h, w16_t, row_starts, out_dtype=F32, name="in_proj", **DENSE_TILES["in_proj"])
    o_a = _nsa(proj, small, nsa_q_gain, nsa_kc_gain, nsa_ks_gain, nsa_kw_gain, cmp_pe_k, cmp_pe_v, cmp_wk1, cmp_wk2,
               cmp_wv1, cmp_wv2, batch=batch, seq=seq)
    o_b = _gdn(proj, small, gdn_conv_w, gdn_a_log, gdn_dt_bias, gdn_o_gain, batch=batch, seq=seq)
    mixed = _merge(o_a, w_up_nsa.astype(BF16), o_b, w_up_gdn.astype(BF16), proj, **DENSE_TILES["merge_up"])
    x, xg, ss = _out_proj(mixed, w_out.astype(BF16), x, g_mlp, **DENSE_TILES["out_proj"])
    hidden = _mlp_in(xg, w_mlp_in.astype(BF16), ss, **DENSE_TILES["mlp_in"])
    x, xg, ss = _mlp_out(hidden, w_mlp_out.astype(BF16), x, g_ple, **DENSE_TILES["mlp_out"])
    return _ple(xg, w_ple_gate.astype(BF16), p.astype(BF16), w_ple_proj.astype(BF16), x, ss, **DENSE_TILES["ple"])


def kernel(x, p, g_mix, w_in, nsa_q_gain, nsa_kc_gain, nsa_ks_gain, nsa_kw_gain, cmp_pe_k, cmp_pe_v, cmp_wk1, cmp_wk2,
           cmp_wv1, cmp_wv2, gdn_conv_w, gdn_a_log, gdn_dt_bias, gdn_o_gain, w_up_nsa, w_up_gdn, w_out, g_mlp,
           w_mlp_in, w_mlp_out, g_ple, w_ple_gate, w_ple_proj):
    batch, seq, d = x.shape
    depth = p.shape[0]
    assert seq // SEL_BLOCK <= LANES and seq % (4 * LANES) == 0
    y = x.reshape(batch * seq, d)
    for i in range(depth):
        y = _layer(y, p[i].reshape(batch * seq, -1), g_mix[i], w_in[i], nsa_q_gain[i], nsa_kc_gain[i], nsa_ks_gain[i],
                   nsa_kw_gain[i], cmp_pe_k[i], cmp_pe_v[i], cmp_wk1[i], cmp_wk2[i], cmp_wv1[i], cmp_wv2[i],
                   gdn_conv_w[i], gdn_a_log[i], gdn_dt_bias[i], gdn_o_gain[i], w_up_nsa[i], w_up_gdn[i], w_out[i],
                   g_mlp[i], w_mlp_in[i], w_mlp_out[i], g_ple[i], w_ple_gate[i], w_ple_proj[i], batch=batch, seq=seq)
    return y.reshape(batch, seq, d)
```
